```python
import math
import jax
import jax.numpy as jnp
from jax import lax
import numpy as np

D_MODEL = 4096
BATCH = 4
SEQ = 2048
DEPTH = 2
DEC_BATCH = 8
DEC_SEQ = 4
PAST_LEN = 16384
PAGE_SIZE = 128

BR_W = D_MODEL // 2
LRU_HEADS = 16
LRU_HD = BR_W // LRU_HEADS
LRU_CONV = 4
LRU_C = 8.0
HEAD_DIM = 128
NSA_HEADS = BR_W // HEAD_DIM
NSA_KV = 4
NSA_HG = NSA_HEADS // NSA_KV
ROT_DIM = HEAD_DIM // 4
ROPE_THETA = 500000.0
CMP_LEN = 32
CMP_STRIDE = 16
CMP_HIDDEN = 2 * HEAD_DIM
SEL_BLOCK = 64
N_SEL = 16
WINDOW = 512
SEL_Q_CHUNK = 32
WIN_Q_BLOCK = 128
FORCE_BONUS = 1e4
NEG = -1e30
TINY = 1e-30
RWKV_HD = 64
RWKV_HEADS = BR_W // RWKV_HD
LORA_W = 64
LORA_A = 64
LORA_G = 128
RWKV_SIZES = (BR_W, BR_W, BR_W, LORA_W, LORA_A, LORA_G)
RWKV_IN = sum(RWKV_SIZES)
RWKV_OFFSETS = tuple(int(v) for v in np.cumsum(RWKV_SIZES)[:-1])
GN_EPS = 64e-5
D_FF = ((8 * D_MODEL // 3 + 255) // 256) * 256
FFN_CONV = 3
NORM_EPS = 1e-6
IN_SIZES = (BR_W, BR_W, NSA_HEADS * HEAD_DIM, 2 * NSA_KV * HEAD_DIM, 2 * NSA_KV * HEAD_DIM,
            2 * NSA_KV * HEAD_DIM, 3 * NSA_HEADS, RWKV_IN, 3 * D_MODEL)
D_IN = sum(IN_SIZES)
IN_OFFSETS = tuple(int(v) for v in np.cumsum(IN_SIZES)[:-1])
STATE_NAMES = ("nsa", "win", "lru_h", "lru_conv", "rwkv_s", "rwkv_shift", "ffn_conv")

kernel_name = "hybrid_rglru_nsa_rwkv7_convffn_step"


def rms_norm(x, g):
    xf = x.astype(jnp.float32)
    y = xf * lax.rsqrt(jnp.mean(xf * xf, axis=-1, keepdims=True) + NORM_EPS)
    return (y * g.astype(jnp.float32)).astype(x.dtype)


def masked_softmax(s, mask):
    s = jnp.where(mask, s.astype(jnp.float32), NEG)
    p = jnp.where(mask, jnp.exp(s - jnp.max(s, axis=-1, keepdims=True)), 0.0)
    return p / jnp.maximum(jnp.sum(p, axis=-1, keepdims=True), TINY)


def partial_rope(x, pos):
    half = ROT_DIM // 2
    inv_freq = jnp.power(ROPE_THETA, -jnp.arange(half, dtype=jnp.float32) / half)
    ang = pos.astype(jnp.float32)[:, None] * inv_freq[None, :]
    cos = jnp.cos(ang)[:, None, :]
    sin = jnp.sin(ang)[:, None, :]
    xr = x[..., :ROT_DIM].astype(jnp.float32)
    x1, x2 = xr[..., :half], xr[..., half:]
    rot = jnp.concatenate([x1 * cos - x2 * sin, x2 * cos + x1 * sin], axis=-1)
    return jnp.concatenate([rot.astype(x.dtype), x[..., ROT_DIM:]], axis=-1)


def causal_dwconv(x, buf, w, b):
    width = w.shape[0]
    n = x.shape[1]
    xx = jnp.concatenate([buf.astype(x.dtype), x], axis=1)
    y = b + sum(xx[:, j:j + n] * w[j] for j in range(width))
    return y, xx[:, xx.shape[1] - (width - 1):]


def linear_combine(e1, e2):
    a1, b1 = e1
    a2, b2 = e2
    return a1 * a2, a2 * b1 + b2


def rglru_mixer(x_in, gate_in, conv_buf, h0, lp):
    b, n, _ = x_in.shape
    f32 = jnp.float32
    xc, conv_new = causal_dwconv(x_in, conv_buf, lp["lru_conv_w"], lp["lru_conv_b"])
    xh = xc.reshape(b, n, LRU_HEADS, LRU_HD)
    r = jax.nn.sigmoid((jnp.einsum("bnhi,hij->bnhj", xh, lp["lru_wa"]).reshape(b, n, BR_W)
                        + lp["lru_ba"]).astype(f32))
    i = jax.nn.sigmoid((jnp.einsum("bnhi,hij->bnhj", xh, lp["lru_wx"]).reshape(b, n, BR_W)
                        + lp["lru_bx"]).astype(f32))
    log_a = -LRU_C * r * jax.nn.softplus(-lp["lru_lambda"].astype(f32))
    a = jnp.exp(log_a)
    u = jnp.sqrt(-jnp.expm1(2.0 * log_a)) * (i * xc.astype(f32))
    u = u.at[:, 0].add(a[:, 0] * h0.astype(f32))
    _, h = lax.associative_scan(linear_combine, (a, u), axis=1)
    y = (h * jax.nn.gelu(gate_in.astype(f32))).astype(x_in.dtype)
    return y, conv_new, h[:, -1].astype(x_in.dtype)


def compress_blocks(rows, pe, w1, w2):
    b, length, g, d = rows.shape
    chunks = rows.reshape(b, length // CMP_STRIDE, CMP_STRIDE, g, d)
    w1r = w1.reshape(CMP_LEN, d, CMP_HIDDEN)
    first = jnp.einsum("bcpgd,pde->bcge", chunks, w1r[:CMP_STRIDE])
    second = jnp.einsum("bcpgd,pde->bcge", chunks, w1r[CMP_STRIDE:])
    pe_term = jnp.einsum("pd,pde->e", pe, w1r)
    hidden = jax.nn.gelu(first[:, :-1] + second[:, 1:] + pe_term)
    return jnp.einsum("bcge,ed->bcgd", hidden, w2)


def nsa_mixer(q, rows, win_kv, gates, q_off, lp):
    b, n = q.shape[0], q.shape[1]
    lpad = rows.shape[1]
    scale = HEAD_DIM ** -0.5
    tq = q_off + jnp.arange(n)
    q5 = q.reshape(b, n, NSA_KV, NSA_HG, HEAD_DIM)
    k_cmp = compress_blocks(rows[:, :, 0], lp["cmp_pe"][0], lp["cmp_w1"][0], lp["cmp_w2"][0])
    v_cmp = compress_blocks(rows[:, :, 1], lp["cmp_pe"][1], lp["cmp_w1"][1], lp["cmp_w2"][1])
    n_cmp = k_cmp.shape[1]
    cmp_end = jnp.arange(n_cmp) * CMP_STRIDE + (CMP_LEN - 1)
    k_cmp = partial_rope(rms_norm(k_cmp, lp["qk_norm"][1]), cmp_end)
    s_c = jnp.einsum("bqghd,bkgd->bghqk", q5, k_cmp) * scale
    p_c = masked_softmax(s_c, cmp_end[None, :] <= tq[:, None])
    o_c = jnp.einsum("bghqk,bkgd->bqghd", p_c.astype(q.dtype), v_cmp)
    n_slc = lpad // SEL_BLOCK
    ratio = SEL_BLOCK // CMP_STRIDE
    imp = jnp.pad(p_c.sum(axis=2), ((0, 0), (0, 0), (0, 0), (1, 1)))
    imp_slc = (imp[..., :ratio * n_slc].reshape(b, NSA_KV, n, n_slc, ratio).sum(-1)
               + imp[..., ratio::ratio])
    blk = jnp.arange(n_slc)[None, :]
    cur = (tq // SEL_BLOCK)[:, None]
    valid = blk * SEL_BLOCK <= tq[:, None]
    forced = (blk == 0) | (blk == cur) | (blk == cur - 1)
    score = jnp.where(valid, imp_slc + jnp.where(forced, FORCE_BONUS, 0.0), NEG)
    k_sel = min(N_SEL, n_slc)
    top_val, top_idx = lax.top_k(score, k_sel)
    top_ok = top_val > 0.5 * NEG
    ks_blk = rows[:, :, 2].reshape(b, n_slc, SEL_BLOCK, NSA_KV, HEAD_DIM).transpose(0, 3, 1, 2, 4)
    vs_blk = rows[:, :, 3].reshape(b, n_slc, SEL_BLOCK, NSA_KV, HEAD_DIM).transpose(0, 3, 1, 2, 4)
    ks_blk = ks_blk.reshape(b * NSA_KV, n_slc, SEL_BLOCK, HEAD_DIM)
    vs_blk = vs_blk.reshape(b * NSA_KV, n_slc, SEL_BLOCK, HEAD_DIM)
    qc = math.gcd(n, SEL_Q_CHUNK)
    nc = n // qc
    m = k_sel * SEL_BLOCK

    def sel_chunk(args):
        q_c, idx_c, ok_c, t_c = args
        flat = idx_c.reshape(b * NSA_KV, qc * k_sel)
        kg = jax.vmap(lambda blocks, ix: blocks[ix])(ks_blk, flat).reshape(b, NSA_KV, qc, m, HEAD_DIM)
        vg = jax.vmap(lambda blocks, ix: blocks[ix])(vs_blk, flat).reshape(b, NSA_KV, qc, m, HEAD_DIM)
        s = jnp.einsum("bqghd,bgqmd->bghqm", q_c, kg) * scale
        kpos = (idx_c[..., None] * SEL_BLOCK + jnp.arange(SEL_BLOCK)).reshape(b, NSA_KV, qc, m)
        mask = (kpos <= t_c[None, None, :, None]) & jnp.repeat(ok_c, SEL_BLOCK, axis=-1)
        p = masked_softmax(s, mask[:, :, None])
        return jnp.einsum("bghqm,bgqmd->bqghd", p.astype(q_c.dtype), vg)

    chunks = (jnp.moveaxis(q5.reshape(b, nc, qc, NSA_KV, NSA_HG, HEAD_DIM), 1, 0),
              jnp.moveaxis(top_idx.reshape(b, NSA_KV, nc, qc, k_sel), 2, 0),
              jnp.moveaxis(top_ok.reshape(b, NSA_KV, nc, qc, k_sel), 2, 0),
              tq.reshape(nc, qc))
    o_s = jnp.moveaxis(lax.map(sel_chunk, chunks), 0, 1).reshape(b, n, NSA_KV, NSA_HG, HEAD_DIM)
    qb = math.gcd(n, WIN_Q_BLOCK)
    nb = n // qb
    kw, vw = win_kv[:, :, 0], win_kv[:, :, 1]

    def win_block(i):
        start = i * qb
        q_b = lax.dynamic_slice_in_dim(q5, start, qb, axis=1)
        k_b = lax.dynamic_slice_in_dim(kw, start, qb + WINDOW, axis=1)
        v_b = lax.dynamic_slice_in_dim(vw, start, qb + WINDOW, axis=1)
        t_b = q_off + start + jnp.arange(qb)
        kpos = q_off - WINDOW + start + jnp.arange(qb + WINDOW)
        mask = ((kpos[None, :] <= t_b[:, None]) & (kpos[None, :] >= t_b[:, None] - WINDOW)
                & (kpos[None, :] >= 0))
        s = jnp.einsum("bqghd,bkgd->bghqk", q_b, k_b) * scale
        p = masked_softmax(s, mask)
        return jnp.einsum("bghqk,bkgd->bqghd", p.astype(q_b.dtype), v_b)

    o_w = jnp.moveaxis(lax.map(win_block, jnp.arange(nb)), 0, 1).reshape(b, n, NSA_KV, NSA_HG, HEAD_DIM)
    g5 = gates.reshape(b, n, NSA_KV, NSA_HG, 3).astype(q.dtype)
    o = g5[..., 0:1] * o_c + g5[..., 1:2] * o_s + g5[..., 2:3] * o_w
    return o.reshape(b, n, NSA_HEADS * HEAD_DIM)


def rwkv7_mixer(z, shift_buf, s0, lp):
    b, n, _ = z.shape
    f32 = jnp.float32
    prev = jnp.concatenate([shift_buf[:, None].astype(z.dtype), z[:, :-1]], axis=1)
    zm = z + (prev - z) * lp["rwkv_mu"]
    r, k, v, wd, ad, gd = jnp.split(zm, RWKV_OFFSETS, axis=-1)
    w_log = -jax.nn.softplus(-(lp["rwkv_w0"] + jnp.tanh(wd) @ lp["rwkv_w2"]).astype(f32)) - 0.5
    decay = jnp.exp(-jnp.exp(w_log))
    a = jax.nn.sigmoid((lp["rwkv_a0"] + ad @ lp["rwkv_a2"]).astype(f32))
    g = jax.nn.sigmoid(gd) @ lp["rwkv_g2"]
    hs = lambda t: t.astype(f32).reshape(b, n, RWKV_HEADS, RWKV_HD)
    kk = hs(k * lp["rwkv_kk"])
    kk = kk / jnp.maximum(jnp.sqrt(jnp.sum(kk * kk, axis=-1, keepdims=True)), 1e-12)
    k_h = hs(k.astype(f32) * (1.0 + (a - 1.0) * lp["rwkv_ka"].astype(f32)))
    r_h, v_h, w_h, a_h = hs(r), hs(v), hs(decay), hs(a)

    def step(S, inp):
        r_t, w_t, k_t, v_t, kk_t, a_t = inp
        sa = jnp.einsum("bhij,bhj->bhi", S, -kk_t)
        S = (S * w_t[:, :, None, :] + sa[..., None] * (kk_t * a_t)[:, :, None, :]
             + v_t[..., None] * k_t[:, :, None, :])
        return S, jnp.einsum("bhij,bhj->bhi", S, r_t)

    tm = lambda t: jnp.moveaxis(t, 1, 0)
    s_fin, y = lax.scan(step, s0.astype(f32), (tm(r_h), tm(w_h), tm(k_h), tm(v_h), tm(kk), tm(a_h)))
    y = jnp.moveaxis(y, 0, 1)
    mu = jnp.mean(y, axis=-1, keepdims=True)
    var = jnp.mean((y - mu) ** 2, axis=-1, keepdims=True)
    y = ((y - mu) * lax.rsqrt(var + GN_EPS)).reshape(b, n, BR_W) * lp["rwkv_gn_g"] + lp["rwkv_gn_b"]
    bonus = jnp.sum(r_h * k_h * lp["rwkv_rk"].astype(f32), axis=-1, keepdims=True) * v_h
    y = y + bonus.reshape(b, n, BR_W)
    out = (y * g.astype(f32)).astype(z.dtype)
    return out, s_fin.astype(z.dtype), z[:, -1]


def conv_ffn(x, buf, lp):
    h = jnp.einsum("bnd,df->bnf", x, lp["ffn_w_in"])
    hg, hv = jnp.split(h, 2, axis=-1)
    hc, buf_new = causal_dwconv(hg, buf, lp["ffn_conv_w"], lp["ffn_conv_b"])
    return jnp.einsum("bnf,fd->bnd", jax.nn.gelu(hc) * hv, lp["ffn_w_down"]), buf_new


def trunk_layer(x, lp, st, q_off, win_keep):
    b, n, _ = x.shape
    dt = x.dtype
    xn = rms_norm(x, lp["norm1"])
    proj = jnp.einsum("bnd,de->bne", xn, lp["w_in"])
    (lru_x, lru_gate, q, kv_cmp, kv_sel, kv_win, nsa_gate, rwkv_in,
     merge_gate) = jnp.split(proj, IN_OFFSETS, axis=-1)
    y_a, lru_conv_new, lru_h_new = rglru_mixer(lru_x, lru_gate, st["lru_conv"], st["lru_h"], lp)
    pos = q_off + jnp.arange(n)
    qn = lp["qk_norm"]
    q = partial_rope(rms_norm(q.reshape(b, n, NSA_HEADS, HEAD_DIM), qn[0]), pos)
    kv_cmp = kv_cmp.reshape(b, n, 2, NSA_KV, HEAD_DIM)
    kv_sel = kv_sel.reshape(b, n, 2, NSA_KV, HEAD_DIM)
    kv_win = kv_win.reshape(b, n, 2, NSA_KV, HEAD_DIM)
    k_sel = partial_rope(rms_norm(kv_sel[:, :, 0], qn[2]), pos)
    k_win = partial_rope(rms_norm(kv_win[:, :, 0], qn[3]), pos)
    nsa_rows = jnp.stack([kv_cmp[:, :, 0], kv_cmp[:, :, 1], k_sel, kv_sel[:, :, 1]], axis=2)
    win_rows = jnp.stack([k_win, kv_win[:, :, 1]], axis=2)
    total = q_off + n
    padded = -(-total // SEL_BLOCK) * SEL_BLOCK
    rows_all = jnp.concatenate([st["nsa_past"].astype(dt), nsa_rows,
                                jnp.zeros((b, padded - total, 4, NSA_KV, HEAD_DIM), dt)], axis=1)
    win_all = jnp.concatenate([st["win"].astype(dt), win_rows], axis=1)
    win_new = win_all[:, win_all.shape[1] - win_keep:]
    win_full = jnp.pad(win_all, ((0, 0), (WINDOW + n - win_all.shape[1], 0), (0, 0), (0, 0), (0, 0)))
    y_b = nsa_mixer(q, rows_all, win_full, jax.nn.sigmoid(nsa_gate.reshape(b, n, NSA_HEADS, 3)), q_off, lp)
    y_c, rwkv_s_new, rwkv_shift_new = rwkv7_mixer(rwkv_in, st["rwkv_shift"], st["rwkv_s"], lp)
    branches = jnp.stack([y_a, y_b, y_c], axis=2)
    proj_b = jnp.einsum("bnkc,kcd->bnkd", branches, lp["w_branch"])
    gate = jax.nn.sigmoid(merge_gate.reshape(b, n, 3, D_MODEL))
    x = x + jnp.einsum("bnd,de->bne", jnp.sum(gate * proj_b, axis=2), lp["w_out"])
    f, ffn_conv_new = conv_ffn(rms_norm(x, lp["norm2"]), st["ffn_conv"], lp)
    x = x + f
    new_state = {"nsa": nsa_rows, "win": win_new, "lru_h": lru_h_new.astype(dt),
                 "lru_conv": lru_conv_new.astype(dt), "rwkv_s": rwkv_s_new.astype(dt),
                 "rwkv_shift": rwkv_shift_new.astype(dt), "ffn_conv": ffn_conv_new.astype(dt)}
    return x, new_state


def setup_inputs(seed: int = 0) -> dict:
    key = jax.random.key(seed)
    k = jax.random.split(key, 48)
    f32 = jnp.float32
    n_pages = PAST_LEN // PAGE_SIZE
    n_pool = (DEC_BATCH * n_pages * 5) // 4
    w_buf = min(WINDOW, PAST_LEN)

    def nrm(i, shape, s):
        return jax.random.normal(k[i], shape, f32) * s

    def gain(i, shape):
        return 1.0 + 0.1 * jax.random.normal(k[i], shape, f32)

    perm = jax.random.permutation(k[9], n_pool)[: DEC_BATCH * n_pages]
    u = jax.random.uniform(k[19], (DEPTH, BR_W), f32, 0.9, 0.999)
    a_init = u ** (1.0 / LRU_C)
    return {
        "x_prompt": nrm(0, (BATCH, SEQ, D_MODEL), 1.0),
        "x_sample": nrm(1, (DEC_BATCH, DEC_SEQ, D_MODEL), 1.0),
        "cache_nsa_kv": nrm(2, (DEPTH, n_pool, PAGE_SIZE, 4, NSA_KV, HEAD_DIM), 1.0),
        "state_win_kv": nrm(3, (DEPTH, DEC_BATCH, w_buf, 2, NSA_KV, HEAD_DIM), 1.0),
        "state_lru_h": nrm(4, (DEPTH, DEC_BATCH, BR_W), 0.5),
        "state_lru_conv": nrm(5, (DEPTH, DEC_BATCH, LRU_CONV - 1, BR_W), 1.0),
        "state_rwkv_s": nrm(6, (DEPTH, DEC_BATCH, RWKV_HEADS, RWKV_HD, RWKV_HD), 0.3),
        "state_rwkv_shift": nrm(7, (DEPTH, DEC_BATCH, RWKV_IN), 1.0),
        "state_ffn_conv": nrm(8, (DEPTH, DEC_BATCH, FFN_CONV - 1, D_FF), 1.0),
        "page_table": perm.reshape(DEC_BATCH, n_pages).astype(jnp.int32),
        "norm1_g": gain(10, (DEPTH, D_MODEL)),
        "norm2_g": gain(11, (DEPTH, D_MODEL)),
        "w_in": nrm(12, (DEPTH, D_MODEL, D_IN), D_MODEL ** -0.5),
        "lru_conv_w": nrm(13, (DEPTH, LRU_CONV, BR_W), LRU_CONV ** -0.5),
        "lru_conv_b": nrm(14, (DEPTH, BR_W), 0.01),
        "lru_wa": nrm(15, (DEPTH, LRU_HEADS, LRU_HD, LRU_HD), LRU_HD ** -0.5),
        "lru_ba": nrm(16, (DEPTH, BR_W), 0.01),
        "lru_wx": nrm(17, (DEPTH, LRU_HEADS, LRU_HD, LRU_HD), LRU_HD ** -0.5),
        "lru_bx": nrm(18, (DEPTH, BR_W), 0.01),
        "lru_lambda": jnp.log(a_init) - jnp.log1p(-a_init),
        "qk_norm_g": gain(20, (DEPTH, 4, HEAD_DIM)),
        "cmp_pe": nrm(21, (DEPTH, 2, CMP_LEN, HEAD_DIM), 0.1),
        "cmp_w1": nrm(22, (DEPTH, 2, CMP_LEN * HEAD_DIM, CMP_HIDDEN), (CMP_LEN * HEAD_DIM) ** -0.5),
        "cmp_w2": nrm(23, (DEPTH, 2, CMP_HIDDEN, HEAD_DIM), CMP_HIDDEN ** -0.5),
        "rwkv_mu": jax.random.uniform(k[24], (DEPTH, RWKV_IN), f32),
        "rwkv_w0": nrm(25, (DEPTH, BR_W), 0.5),
        "rwkv_w2": nrm(26, (DEPTH, LORA_W, BR_W), 0.5 * LORA_W ** -0.5),
        "rwkv_a0": nrm(27, (DEPTH, BR_W), 0.5),
        "rwkv_a2": nrm(28, (DEPTH, LORA_A, BR_W), 0.5 * LORA_A ** -0.5),
        "rwkv_g2": nrm(29, (DEPTH, LORA_G, BR_W), LORA_G ** -0.5),
        "rwkv_kk": gain(30, (DEPTH, BR_W)),
        "rwkv_ka": gain(31, (DEPTH, BR_W)),
        "rwkv_rk": nrm(32, (DEPTH, RWKV_HEADS, RWKV_HD), 0.1),
        "rwkv_gn_g": gain(33, (DEPTH, BR_W)),
        "rwkv_gn_b": nrm(34, (DEPTH, BR_W), 0.01),
        "w_branch": nrm(35, (DEPTH, 3, BR_W, D_MODEL), BR_W ** -0.5),
        "w_out": nrm(36, (DEPTH, D_MODEL, D_MODEL), D_MODEL ** -0.5),
        "ffn_w_in": nrm(37, (DEPTH, D_MODEL, 2 * D_FF), D_MODEL ** -0.5),
        "ffn_conv_w": nrm(38, (DEPTH, FFN_CONV, D_FF), FFN_CONV ** -0.5),
        "ffn_conv_b": nrm(39, (DEPTH, D_FF), 0.01),
        "ffn_w_down": nrm(40, (DEPTH, D_FF, D_MODEL), D_FF ** -0.5),
    }


def reference(x_prompt, x_sample, cache_nsa_kv, state_win_kv, state_lru_h, state_lru_conv,
              state_rwkv_s, state_rwkv_shift, state_ffn_conv, page_table,
              norm1_g, norm2_g, w_in, lru_conv_w, lru_conv_b, lru_wa, lru_ba, lru_wx, lru_bx,
              lru_lambda, qk_norm_g, cmp_pe, cmp_w1, cmp_w2, rwkv_mu, rwkv_w0, rwkv_w2,
              rwkv_a0, rwkv_a2, rwkv_g2, rwkv_kk, rwkv_ka, rwkv_rk, rwkv_gn_g, rwkv_gn_b,
              w_branch, w_out, ffn_w_in, ffn_conv_w, ffn_conv_b, ffn_w_down):
    dt = x_prompt.dtype
    bp, n_p, _ = x_prompt.shape
    bs = x_sample.shape[0]
    past_len = page_table.shape[1] * PAGE_SIZE
    xp, xs = x_prompt, x_sample
    new_p = {name: [] for name in STATE_NAMES}
    new_s = {name: [] for name in STATE_NAMES}
    for l in range(DEPTH):
        lp = {"norm1": norm1_g[l], "norm2": norm2_g[l], "w_in": w_in[l],
              "lru_conv_w": lru_conv_w[l], "lru_conv_b": lru_conv_b[l], "lru_wa": lru_wa[l],
              "lru_ba": lru_ba[l], "lru_wx": lru_wx[l], "lru_bx": lru_bx[l],
              "lru_lambda": lru_lambda[l], "qk_norm": qk_norm_g[l], "cmp_pe": cmp_pe[l],
              "cmp_w1": cmp_w1[l], "cmp_w2": cmp_w2[l], "rwkv_mu": rwkv_mu[l],
              "rwkv_w0": rwkv_w0[l], "rwkv_w2": rwkv_w2[l], "rwkv_a0": rwkv_a0[l],
              "rwkv_a2": rwkv_a2[l], "rwkv_g2": rwkv_g2[l], "rwkv_kk": rwkv_kk[l],
              "rwkv_ka": rwkv_ka[l], "rwkv_rk": rwkv_rk[l], "rwkv_gn_g": rwkv_gn_g[l],
              "rwkv_gn_b": rwkv_gn_b[l], "w_branch": w_branch[l], "w_out": w_out[l],
              "ffn_w_in": ffn_w_in[l], "ffn_conv_w": ffn_conv_w[l], "ffn_conv_b": ffn_conv_b[l],
              "ffn_w_down": ffn_w_down[l]}
        st_p = {"nsa_past": jnp.zeros((bp, 0, 4, NSA_KV, HEAD_DIM), dt),
                "win": jnp.zeros((bp, 0, 2, NSA_KV, HEAD_DIM), dt),
                "lru_h": jnp.zeros((bp, BR_W), dt),
                "lru_conv": jnp.zeros((bp, LRU_CONV - 1, BR_W), dt),
                "rwkv_s": jnp.zeros((bp, RWKV_HEADS, RWKV_HD, RWKV_HD), dt),
                "rwkv_shift": jnp.zeros((bp, RWKV_IN), dt),
                "ffn_conv": jnp.zeros((bp, FFN_CONV - 1, D_FF), dt)}
        xp, out_p = trunk_layer(xp, lp, st_p, 0, min(WINDOW, n_p))
        past = cache_nsa_kv[l, page_table].reshape(bs, past_len, 4, NSA_KV, HEAD_DIM)
        st_s = {"nsa_past": past, "win": state_win_kv[l], "lru_h": state_lru_h[l],
                "lru_conv": state_lru_conv[l], "rwkv_s": state_rwkv_s[l],
                "rwkv_shift": state_rwkv_shift[l], "ffn_conv": state_ffn_conv[l]}
        xs, out_s = trunk_layer(xs, lp, st_s, past_len, state_win_kv.shape[2])
        for name in STATE_NAMES:
            new_p[name].append(out_p[name])
            new_s[name].append(out_s[name])
    sp = {name: jnp.stack(new_p[name], axis=0) for name in STATE_NAMES}
    ss = {name: jnp.stack(new_s[name], axis=0) for name in STATE_NAMES}
    return (xp, xs, sp["nsa"], sp["win"], sp["lru_h"], sp["lru_conv"], sp["rwkv_s"],
            sp["rwkv_shift"], sp["ffn_conv"], ss["nsa"], ss["win"], ss["lru_h"], ss["lru_conv"],
            ss["rwkv_s"], ss["rwkv_shift"], ss["ffn_conv"])
```

```python
import functools
import math

import jax
import jax.numpy as jnp
import numpy as np
from jax import lax
from jax.experimental import pallas as pl
from jax.experimental.pallas import tpu as pltpu

F32 = jnp.float32
BF16 = jnp.bfloat16

D_MODEL = 4096
PAGE_SIZE = 128
BR_W = D_MODEL // 2
LRU_HEADS = 16
LRU_HD = BR_W // LRU_HEADS
LRU_CONV = 4
LRU_C = 8.0
HEAD_DIM = 128
NSA_HEADS = BR_W // HEAD_DIM
NSA_KV = 4
NSA_HG = NSA_HEADS // NSA_KV
ROT_DIM = HEAD_DIM // 4
ROPE_THETA = 500000.0
CMP_LEN = 32
CMP_STRIDE = 16
CMP_HIDDEN = 2 * HEAD_DIM
SEL_BLOCK = 64
N_SEL = 16
WINDOW = 512
SEL_Q_CHUNK = 32
WIN_Q_BLOCK = 128
FORCE_BONUS = 1e4
NEG = -1e30
TINY = 1e-30
RWKV_HD = 64
RWKV_HEADS = BR_W // RWKV_HD
LORA_W = 64
LORA_A = 64
LORA_G = 128
RWKV_SIZES = (BR_W, BR_W, BR_W, LORA_W, LORA_A, LORA_G)
RWKV_IN = sum(RWKV_SIZES)
RWKV_OFFSETS = tuple(int(v) for v in np.cumsum(RWKV_SIZES)[:-1])
GN_EPS = 64e-5
D_FF = ((8 * D_MODEL // 3 + 255) // 256) * 256
FFN_CONV = 3
NORM_EPS = 1e-6
IN_SIZES = (BR_W, BR_W, NSA_HEADS * HEAD_DIM, 2 * NSA_KV * HEAD_DIM, 2 * NSA_KV * HEAD_DIM,
            2 * NSA_KV * HEAD_DIM, 3 * NSA_HEADS, RWKV_IN, 3 * D_MODEL)
D_IN = sum(IN_SIZES)
IN_OFFSETS = tuple(int(v) for v in np.cumsum(IN_SIZES)[:-1])
STATE_NAMES = ("nsa", "win", "lru_h", "lru_conv", "rwkv_s", "rwkv_shift", "ffn_conv")

VMEM_LIMIT_BYTES = 56 * 1024 * 1024


def _mm_kernel(x_ref, w_ref, o_ref, *, nk):
    part = jnp.dot(x_ref[...], w_ref[...], preferred_element_type=F32)
    if nk == 1:
        o_ref[...] = part
    else:
        k = pl.program_id(2)

        @pl.when(k == 0)
        def _():
            o_ref[...] = part

        @pl.when(k != 0)
        def _():
            o_ref[...] += part


def _matmul(x, w, *, tm, tn, tk=None):
    m, kdim = x.shape
    n = w.shape[1]
    tk = kdim if tk is None else tk
    assert m % tm == 0 and kdim % tk == 0
    nk = kdim // tk
    grid = (m // tm, pl.cdiv(n, tn), nk)
    return pl.pallas_call(
        functools.partial(_mm_kernel, nk=nk),
        grid=grid,
        in_specs=[pl.BlockSpec((tm, tk), lambda i, j, k: (i, k)),
                  pl.BlockSpec((tk, tn), lambda i, j, k: (k, j))],
        out_specs=pl.BlockSpec((tm, tn), lambda i, j, k: (i, j)),
        out_shape=jax.ShapeDtypeStruct((m, n), F32),
        compiler_params=pltpu.CompilerParams(
            dimension_semantics=("arbitrary", "arbitrary", "arbitrary"),
            vmem_limit_bytes=VMEM_LIMIT_BYTES),
    )(x, w)


def _proj(x2d, w_bf16):
    m, kdim = x2d.shape
    xb = x2d.astype(BF16)
    if m >= 1024:
        tm, tn = 1024, 512
        tk = kdim if kdim <= 4096 else kdim // 2
    else:
        tm, tn = m, 1024
        tk = kdim if kdim <= 4096 else kdim // 2
    return _matmul(xb, w_bf16, tm=tm, tn=tn, tk=tk)


def _rms_norm(x, g):
    xf = x.astype(F32)
    y = xf * lax.rsqrt(jnp.mean(xf * xf, axis=-1, keepdims=True) + NORM_EPS)
    return (y * g.astype(F32)).astype(x.dtype)


def _masked_softmax(s, mask):
    s = jnp.where(mask, s.astype(F32), NEG)
    p = jnp.where(mask, jnp.exp(s - jnp.max(s, axis=-1, keepdims=True)), 0.0)
    return p / jnp.maximum(jnp.sum(p, axis=-1, keepdims=True), TINY)


def _partial_rope(x, pos):
    half = ROT_DIM // 2
    inv_freq = jnp.power(ROPE_THETA, -jnp.arange(half, dtype=F32) / half)
    ang = pos.astype(F32)[:, None] * inv_freq[None, :]
    cos = jnp.cos(ang)[:, None, :]
    sin = jnp.sin(ang)[:, None, :]
    xr = x[..., :ROT_DIM].astype(F32)
    x1, x2 = xr[..., :half], xr[..., half:]
    rot = jnp.concatenate([x1 * cos - x2 * sin, x2 * cos + x1 * sin], axis=-1)
    return jnp.concatenate([rot.astype(x.dtype), x[..., ROT_DIM:]], axis=-1)


def _causal_dwconv(x, buf, w, b):
    width = w.shape[0]
    n = x.shape[1]
    xx = jnp.concatenate([buf.astype(x.dtype), x], axis=1)
    y = b + sum(xx[:, j:j + n] * w[j] for j in range(width))
    return y, xx[:, xx.shape[1] - (width - 1):]


def _linear_combine(e1, e2):
    a1, b1 = e1
    a2, b2 = e2
    return a1 * a2, a2 * b1 + b2


def _rglru_mixer(x_in, gate_in, conv_buf, h0, lp):
    b, n, _ = x_in.shape
    xc, conv_new = _causal_dwconv(x_in, conv_buf, lp["lru_conv_w"], lp["lru_conv_b"])
    xh = xc.reshape(b, n, LRU_HEADS, LRU_HD)
    r = jax.nn.sigmoid((jnp.einsum("bnhi,hij->bnhj", xh, lp["lru_wa"]).reshape(b, n, BR_W)
                        + lp["lru_ba"]).astype(F32))
    i = jax.nn.sigmoid((jnp.einsum("bnhi,hij->bnhj", xh, lp["lru_wx"]).reshape(b, n, BR_W)
                        + lp["lru_bx"]).astype(F32))
    log_a = -LRU_C * r * jax.nn.softplus(-lp["lru_lambda"].astype(F32))
    a = jnp.exp(log_a)
    u = jnp.sqrt(-jnp.expm1(2.0 * log_a)) * (i * xc.astype(F32))
    u = u.at[:, 0].add(a[:, 0] * h0.astype(F32))
    _, h = lax.associative_scan(_linear_combine, (a, u), axis=1)
    y = (h * jax.nn.gelu(gate_in.astype(F32))).astype(x_in.dtype)
    return y, conv_new, h[:, -1].astype(x_in.dtype)


def _compress_blocks(rows, pe, w1, w2):
    b, length, g, d = rows.shape
    chunks = rows.reshape(b, length // CMP_STRIDE, CMP_STRIDE, g, d)
    w1r = w1.reshape(CMP_LEN, d, CMP_HIDDEN)
    first = jnp.einsum("bcpgd,pde->bcge", chunks, w1r[:CMP_STRIDE])
    second = jnp.einsum("bcpgd,pde->bcge", chunks, w1r[CMP_STRIDE:])
    pe_term = jnp.einsum("pd,pde->e", pe, w1r)
    hidden = jax.nn.gelu(first[:, :-1] + second[:, 1:] + pe_term)
    return jnp.einsum("bcge,ed->bcgd", hidden, w2)


def _nsa_mixer(q, rows, win_kv, gates, q_off, lp):
    b, n = q.shape[0], q.shape[1]
    lpad = rows.shape[1]
    scale = HEAD_DIM ** -0.5
    tq = q_off + jnp.arange(n)
    q5 = q.reshape(b, n, NSA_KV, NSA_HG, HEAD_DIM)
    k_cmp = _compress_blocks(rows[:, :, 0], lp["cmp_pe"][0], lp["cmp_w1"][0], lp["cmp_w2"][0])
    v_cmp = _compress_blocks(rows[:, :, 1], lp["cmp_pe"][1], lp["cmp_w1"][1], lp["cmp_w2"][1])
    n_cmp = k_cmp.shape[1]
    cmp_end = jnp.arange(n_cmp) * CMP_STRIDE + (CMP_LEN - 1)
    k_cmp = _partial_rope(_rms_norm(k_cmp, lp["qk_norm"][1]), cmp_end)
    s_c = jnp.einsum("bqghd,bkgd->bghqk", q5, k_cmp) * scale
    p_c = _masked_softmax(s_c, cmp_end[None, :] <= tq[:, None])
    o_c = jnp.einsum("bghqk,bkgd->bqghd", p_c.astype(q.dtype), v_cmp)
    n_slc = lpad // SEL_BLOCK
    ratio = SEL_BLOCK // CMP_STRIDE
    imp = jnp.pad(p_c.sum(axis=2), ((0, 0), (0, 0), (0, 0), (1, 1)))
    imp_slc = (imp[..., :ratio * n_slc].reshape(b, NSA_KV, n, n_slc, ratio).sum(-1)
               + imp[..., ratio::ratio])
    blk = jnp.arange(n_slc)[None, :]
    cur = (tq // SEL_BLOCK)[:, None]
    valid = blk * SEL_BLOCK <= tq[:, None]
    forced = (blk == 0) | (blk == cur) | (blk == cur - 1)
    score = jnp.where(valid, imp_slc + jnp.where(forced, FORCE_BONUS, 0.0), NEG)
    k_sel = min(N_SEL, n_slc)
    top_val, top_idx = lax.top_k(score, k_sel)
    top_ok = top_val > 0.5 * NEG
    ks_blk = rows[:, :, 2].reshape(b, n_slc, SEL_BLOCK, NSA_KV, HEAD_DIM).transpose(0, 3, 1, 2, 4)
    vs_blk = rows[:, :, 3].reshape(b, n_slc, SEL_BLOCK, NSA_KV, HEAD_DIM).transpose(0, 3, 1, 2, 4)
    ks_blk = ks_blk.reshape(b * NSA_KV, n_slc, SEL_BLOCK, HEAD_DIM)
    vs_blk = vs_blk.reshape(b * NSA_KV, n_slc, SEL_BLOCK, HEAD_DIM)
    qc = math.gcd(n, SEL_Q_CHUNK)
    nc = n // qc
    m = k_sel * SEL_BLOCK

    def sel_chunk(args):
        q_c, idx_c, ok_c, t_c = args
        flat = idx_c.reshape(b * NSA_KV, qc * k_sel)
        kg = jax.vmap(lambda blocks, ix: blocks[ix])(ks_blk, flat).reshape(b, NSA_KV, qc, m, HEAD_DIM)
        vg = jax.vmap(lambda blocks, ix: blocks[ix])(vs_blk, flat).reshape(b, NSA_KV, qc, m, HEAD_DIM)
        s = jnp.einsum("bqghd,bgqmd->bghqm", q_c, kg) * scale
        kpos = (idx_c[..., None] * SEL_BLOCK + jnp.arange(SEL_BLOCK)).reshape(b, NSA_KV, qc, m)
        mask = (kpos <= t_c[None, None, :, None]) & jnp.repeat(ok_c, SEL_BLOCK, axis=-1)
        p = _masked_softmax(s, mask[:, :, None])
        return jnp.einsum("bghqm,bgqmd->bqghd", p.astype(q_c.dtype), vg)

    chunks = (jnp.moveaxis(q5.reshape(b, nc, qc, NSA_KV, NSA_HG, HEAD_DIM), 1, 0),
              jnp.moveaxis(top_idx.reshape(b, NSA_KV, nc, qc, k_sel), 2, 0),
              jnp.moveaxis(top_ok.reshape(b, NSA_KV, nc, qc, k_sel), 2, 0),
              tq.reshape(nc, qc))
    o_s = jnp.moveaxis(lax.map(sel_chunk, chunks), 0, 1).reshape(b, n, NSA_KV, NSA_HG, HEAD_DIM)
    qb = math.gcd(n, WIN_Q_BLOCK)
    nb = n // qb
    kw, vw = win_kv[:, :, 0], win_kv[:, :, 1]

    def win_block(i):
        start = i * qb
        q_b = lax.dynamic_slice_in_dim(q5, start, qb, axis=1)
        k_b = lax.dynamic_slice_in_dim(kw, start, qb + WINDOW, axis=1)
        v_b = lax.dynamic_slice_in_dim(vw, start, qb + WINDOW, axis=1)
        t_b = q_off + start + jnp.arange(qb)
        kpos = q_off - WINDOW + start + jnp.arange(qb + WINDOW)
        mask = ((kpos[None, :] <= t_b[:, None]) & (kpos[None, :] >= t_b[:, None] - WINDOW)
                & (kpos[None, :] >= 0))
        s = jnp.einsum("bqghd,bkgd->bghqk", q_b, k_b) * scale
        p = _masked_softmax(s, mask)
        return jnp.einsum("bghqk,bkgd->bqghd", p.astype(q_b.dtype), v_b)

    o_w = jnp.moveaxis(lax.map(win_block, jnp.arange(nb)), 0, 1).reshape(b, n, NSA_KV, NSA_HG, HEAD_DIM)
    g5 = gates.reshape(b, n, NSA_KV, NSA_HG, 3).astype(q.dtype)
    o = g5[..., 0:1] * o_c + g5[..., 1:2] * o_s + g5[..., 2:3] * o_w
    return o.reshape(b, n, NSA_HEADS * HEAD_DIM)


def _rwkv7_mixer(z, shift_buf, s0, lp):
    b, n, _ = z.shape
    prev = jnp.concatenate([shift_buf[:, None].astype(z.dtype), z[:, :-1]], axis=1)
    zm = z + (prev - z) * lp["rwkv_mu"]
    r, k, v, wd, ad, gd = jnp.split(zm, RWKV_OFFSETS, axis=-1)
    w_log = -jax.nn.softplus(-(lp["rwkv_w0"] + jnp.tanh(wd) @ lp["rwkv_w2"]).astype(F32)) - 0.5
    decay = jnp.exp(-jnp.exp(w_log))
    a = jax.nn.sigmoid((lp["rwkv_a0"] + ad @ lp["rwkv_a2"]).astype(F32))
    g = jax.nn.sigmoid(gd) @ lp["rwkv_g2"]
    hs = lambda t: t.astype(F32).reshape(b, n, RWKV_HEADS, RWKV_HD)
    kk = hs(k * lp["rwkv_kk"])
    kk = kk / jnp.maximum(jnp.sqrt(jnp.sum(kk * kk, axis=-1, keepdims=True)), 1e-12)
    k_h = hs(k.astype(F32) * (1.0 + (a - 1.0) * lp["rwkv_ka"].astype(F32)))
    r_h, v_h, w_h, a_h = hs(r), hs(v), hs(decay), hs(a)

    def step(S, inp):
        r_t, w_t, k_t, v_t, kk_t, a_t = inp
        sa = jnp.einsum("bhij,bhj->bhi", S, -kk_t)
        S = (S * w_t[:, :, None, :] + sa[..., None] * (kk_t * a_t)[:, :, None, :]
             + v_t[..., None] * k_t[:, :, None, :])
        return S, jnp.einsum("bhij,bhj->bhi", S, r_t)

    tm = lambda t: jnp.moveaxis(t, 1, 0)
    s_fin, y = lax.scan(step, s0.astype(F32), (tm(r_h), tm(w_h), tm(k_h), tm(v_h), tm(kk), tm(a_h)))
    y = jnp.moveaxis(y, 0, 1)
    mu = jnp.mean(y, axis=-1, keepdims=True)
    var = jnp.mean((y - mu) ** 2, axis=-1, keepdims=True)
    y = ((y - mu) * lax.rsqrt(var + GN_EPS)).reshape(b, n, BR_W) * lp["rwkv_gn_g"] + lp["rwkv_gn_b"]
    bonus = jnp.sum(r_h * k_h * lp["rwkv_rk"].astype(F32), axis=-1, keepdims=True) * v_h
    y = y + bonus.reshape(b, n, BR_W)
    out = (y * g.astype(F32)).astype(z.dtype)
    return out, s_fin.astype(z.dtype), z[:, -1]


def _trunk_layer(x, lp, st, q_off, win_keep):
    b, n, _ = x.shape
    dt = x.dtype
    t = b * n
    xn = _rms_norm(x, lp["norm1"])
    proj = _proj(xn.reshape(t, D_MODEL), lp["w_in"]).reshape(b, n, D_IN)
    (lru_x, lru_gate, q, kv_cmp, kv_sel, kv_win, nsa_gate, rwkv_in,
     merge_gate) = jnp.split(proj, IN_OFFSETS, axis=-1)
    y_a, lru_conv_new, lru_h_new = _rglru_mixer(lru_x, lru_gate, st["lru_conv"], st["lru_h"], lp)
    pos = q_off + jnp.arange(n)
    qn = lp["qk_norm"]
    q = _partial_rope(_rms_norm(q.reshape(b, n, NSA_HEADS, HEAD_DIM), qn[0]), pos)
    kv_cmp = kv_cmp.reshape(b, n, 2, NSA_KV, HEAD_DIM)
    kv_sel = kv_sel.reshape(b, n, 2, NSA_KV, HEAD_DIM)
    kv_win = kv_win.reshape(b, n, 2, NSA_KV, HEAD_DIM)
    k_sel = _partial_rope(_rms_norm(kv_sel[:, :, 0], qn[2]), pos)
    k_win = _partial_rope(_rms_norm(kv_win[:, :, 0], qn[3]), pos)
    nsa_rows = jnp.stack([kv_cmp[:, :, 0], kv_cmp[:, :, 1], k_sel, kv_sel[:, :, 1]], axis=2)
    win_rows = jnp.stack([k_win, kv_win[:, :, 1]], axis=2)
    total = q_off + n
    padded = -(-total // SEL_BLOCK) * SEL_BLOCK
    rows_all = jnp.concatenate([st["nsa_past"].astype(dt), nsa_rows,
                                jnp.zeros((b, padded - total, 4, NSA_KV, HEAD_DIM), dt)], axis=1)
    win_all = jnp.concatenate([st["win"].astype(dt), win_rows], axis=1)
    win_new = win_all[:, win_all.shape[1] - win_keep:]
    win_full = jnp.pad(win_all, ((0, 0), (WINDOW + n - win_all.shape[1], 0), (0, 0), (0, 0), (0, 0)))
    y_b = _nsa_mixer(q, rows_all, win_full, jax.nn.sigmoid(nsa_gate.reshape(b, n, NSA_HEADS, 3)), q_off, lp)
    y_c, rwkv_s_new, rwkv_shift_new = _rwkv7_mixer(rwkv_in, st["rwkv_shift"], st["rwkv_s"], lp)
    gate = jax.nn.sigmoid(merge_gate.reshape(b, n, 3, D_MODEL))
    mix = 0.0
    for bi, y in enumerate((y_a, y_b, y_c)):
        pb = _proj(y.reshape(t, BR_W), lp["w_branch"][bi]).reshape(b, n, D_MODEL)
        mix = mix + gate[:, :, bi] * pb
    x = x + _proj(mix.reshape(t, D_MODEL), lp["w_out"]).reshape(b, n, D_MODEL)
    xn2 = _rms_norm(x, lp["norm2"])
    h = _proj(xn2.reshape(t, D_MODEL), lp["ffn_w_in"]).reshape(b, n, 2 * D_FF)
    hg, hv = jnp.split(h, 2, axis=-1)
    hc, ffn_conv_new = _causal_dwconv(hg, st["ffn_conv"], lp["ffn_conv_w"], lp["ffn_conv_b"])
    act = jax.nn.gelu(hc) * hv
    x = x + _proj(act.reshape(t, D_FF), lp["ffn_w_down"]).reshape(b, n, D_MODEL)
    new_state = {"nsa": nsa_rows, "win": win_new, "lru_h": lru_h_new.astype(dt),
                 "lru_conv": lru_conv_new.astype(dt), "rwkv_s": rwkv_s_new.astype(dt),
                 "rwkv_shift": rwkv_shift_new.astype(dt), "ffn_conv": ffn_conv_new.astype(dt)}
    return x, new_state


def kernel(x_prompt, x_sample, cache_nsa_kv, state_win_kv, state_lru_h, state_lru_conv,
           state_rwkv_s, state_rwkv_shift, state_ffn_conv, page_table,
           norm1_g, norm2_g, w_in, lru_conv_w, lru_conv_b, lru_wa, lru_ba, lru_wx, lru_bx,
           lru_lambda, qk_norm_g, cmp_pe, cmp_w1, cmp_w2, rwkv_mu, rwkv_w0, rwkv_w2,
           rwkv_a0, rwkv_a2, rwkv_g2, rwkv_kk, rwkv_ka, rwkv_rk, rwkv_gn_g, rwkv_gn_b,
           w_branch, w_out, ffn_w_in, ffn_conv_w, ffn_conv_b, ffn_w_down):
    dt = x_prompt.dtype
    bp, n_p, _ = x_prompt.shape
    bs = x_sample.shape[0]
    depth = w_in.shape[0]
    past_len = page_table.shape[1] * PAGE_SIZE
    xp, xs = x_prompt, x_sample
    new_p = {name: [] for name in STATE_NAMES}
    new_s = {name: [] for name in STATE_NAMES}
    for l in range(depth):
        lp = {"norm1": norm1_g[l], "norm2": norm2_g[l], "w_in": w_in[l].astype(BF16),
              "lru_conv_w": lru_conv_w[l], "lru_conv_b": lru_conv_b[l], "lru_wa": lru_wa[l],
              "lru_ba": lru_ba[l], "lru_wx": lru_wx[l], "lru_bx": lru_bx[l],
              "lru_lambda": lru_lambda[l], "qk_norm": qk_norm_g[l], "cmp_pe": cmp_pe[l],
              "cmp_w1": cmp_w1[l], "cmp_w2": cmp_w2[l], "rwkv_mu": rwkv_mu[l],
              "rwkv_w0": rwkv_w0[l], "rwkv_w2": rwkv_w2[l], "rwkv_a0": rwkv_a0[l],
              "rwkv_a2": rwkv_a2[l], "rwkv_g2": rwkv_g2[l], "rwkv_kk": rwkv_kk[l],
              "rwkv_ka": rwkv_ka[l], "rwkv_rk": rwkv_rk[l], "rwkv_gn_g": rwkv_gn_g[l],
              "rwkv_gn_b": rwkv_gn_b[l], "w_branch": w_branch[l].astype(BF16),
              "w_out": w_out[l].astype(BF16), "ffn_w_in": ffn_w_in[l].astype(BF16),
              "ffn_conv_w": ffn_conv_w[l], "ffn_conv_b": ffn_conv_b[l],
              "ffn_w_down": ffn_w_down[l].astype(BF16)}
        st_p = {"nsa_past": jnp.zeros((bp, 0, 4, NSA_KV, HEAD_DIM), dt),
                "win": jnp.zeros((bp, 0, 2, NSA_KV, HEAD_DIM), dt),
                "lru_h": jnp.zeros((bp, BR_W), dt),
                "lru_conv": jnp.zeros((bp, LRU_CONV - 1, BR_W), dt),
                "rwkv_s": jnp.zeros((bp, RWKV_HEADS, RWKV_HD, RWKV_HD), dt),
                "rwkv_shift": jnp.zeros((bp, RWKV_IN), dt),
                "ffn_conv": jnp.zeros((bp, FFN_CONV - 1, D_FF), dt)}
        xp, out_p = _trunk_layer(xp, lp, st_p, 0, min(WINDOW, n_p))
        past = cache_nsa_kv[l, page_table].reshape(bs, past_len, 4, NSA_KV, HEAD_DIM)
        st_s = {"nsa_past": past, "win": state_win_kv[l], "lru_h": state_lru_h[l],
                "lru_conv": state_lru_conv[l], "rwkv_s": state_rwkv_s[l],
                "rwkv_shift": state_rwkv_shift[l], "ffn_conv": state_ffn_conv[l]}
        xs, out_s = _trunk_layer(xs, lp, st_s, past_len, state_win_kv.shape[2])
        for name in STATE_NAMES:
            new_p[name].append(out_p[name])
            new_s[name].append(out_s[name])
    sp = {name: jnp.stack(new_p[name], axis=0) for name in STATE_NAMES}
    ss = {name: jnp.stack(new_s[name], axis=0) for name in STATE_NAMES}
    return (xp, xs, sp["nsa"], sp["win"], sp["lru_h"], sp["lru_conv"], sp["rwkv_s"],
            sp["rwkv_shift"], sp["ffn_conv"], ss["nsa"], ss["win"], ss["lru_h"], ss["lru_conv"],
            ss["rwkv_s"], ss["rwkv_shift"], ss["ffn_conv"])
```

```python
import functools
import math

import jax
import jax.numpy as jnp
import numpy as np
from jax import lax
from jax.experimental import pallas as pl
from jax.experimental.pallas import tpu as pltpu

F32 = jnp.float32
BF16 = jnp.bfloat16

D_MODEL = 4096
PAGE_SIZE = 128
BR_W = D_MODEL // 2
LRU_HEADS = 16
LRU_HD = BR_W // LRU_HEADS
LRU_CONV = 4
LRU_C = 8.0
HEAD_DIM = 128
NSA_HEADS = BR_W // HEAD_DIM
NSA_KV = 4
NSA_HG = NSA_HEADS // NSA_KV
ROT_DIM = HEAD_DIM // 4
ROPE_THETA = 500000.0
CMP_LEN = 32
CMP_STRIDE = 16
CMP_HIDDEN = 2 * HEAD_DIM
SEL_BLOCK = 64
N_SEL = 16
WINDOW = 512
SEL_Q_CHUNK = 32
WIN_Q_BLOCK = 128
FORCE_BONUS = 1e4
NEG = -1e30
TINY = 1e-30
RWKV_HD = 64
RWKV_HEADS = BR_W // RWKV_HD
LORA_W = 64
LORA_A = 64
LORA_G = 128
RWKV_SIZES = (BR_W, BR_W, BR_W, LORA_W, LORA_A, LORA_G)
RWKV_IN = sum(RWKV_SIZES)
RWKV_OFFSETS = tuple(int(v) for v in np.cumsum(RWKV_SIZES)[:-1])
GN_EPS = 64e-5
D_FF = ((8 * D_MODEL // 3 + 255) // 256) * 256
FFN_CONV = 3
NORM_EPS = 1e-6
IN_SIZES = (BR_W, BR_W, NSA_HEADS * HEAD_DIM, 2 * NSA_KV * HEAD_DIM, 2 * NSA_KV * HEAD_DIM,
            2 * NSA_KV * HEAD_DIM, 3 * NSA_HEADS, RWKV_IN, 3 * D_MODEL)
D_IN = sum(IN_SIZES)
IN_OFFSETS = tuple(int(v) for v in np.cumsum(IN_SIZES)[:-1])
STATE_NAMES = ("nsa", "win", "lru_h", "lru_conv", "rwkv_s", "rwkv_shift", "ffn_conv")

VMEM_LIMIT_BYTES = 56 * 1024 * 1024


def _mm_kernel(x_ref, w_ref, o_ref, *, nk):
    part = jnp.dot(x_ref[...], w_ref[...], preferred_element_type=F32)
    if nk == 1:
        o_ref[...] = part
    else:
        k = pl.program_id(2)

        @pl.when(k == 0)
        def _():
            o_ref[...] = part

        @pl.when(k != 0)
        def _():
            o_ref[...] += part


def _matmul(x, w, *, tm, tn, tk=None):
    m, kdim = x.shape
    n = w.shape[1]
    tk = kdim if tk is None else tk
    assert m % tm == 0 and kdim % tk == 0
    nk = kdim // tk
    grid = (m // tm, pl.cdiv(n, tn), nk)
    return pl.pallas_call(
        functools.partial(_mm_kernel, nk=nk),
        grid=grid,
        in_specs=[pl.BlockSpec((tm, tk), lambda i, j, k: (i, k)),
                  pl.BlockSpec((tk, tn), lambda i, j, k: (k, j))],
        out_specs=pl.BlockSpec((tm, tn), lambda i, j, k: (i, j)),
        out_shape=jax.ShapeDtypeStruct((m, n), F32),
        compiler_params=pltpu.CompilerParams(
            dimension_semantics=("arbitrary", "arbitrary", "arbitrary"),
            vmem_limit_bytes=VMEM_LIMIT_BYTES),
    )(x, w)


def _proj(x2d, w_bf16):
    m, kdim = x2d.shape
    n = w_bf16.shape[1]
    xb = x2d.astype(BF16)
    tm, tn = (1024, 512) if m >= 1024 else (m, 1024)
    tn = min(tn, n)
    tk = kdim if kdim <= 4096 else kdim // 2
    return _matmul(xb, w_bf16, tm=tm, tn=tn, tk=tk)


RWKV_HEAD_BLOCK = 16


def _rwkv_scan_kernel(r_ref, w_ref, k_ref, v_ref, kk_ref, b_ref, s0_ref, y_ref, s_ref, *, tt, hb):
    @pl.when(pl.program_id(2) == 0)
    def _():
        s_ref[...] = s0_ref[...]

    hd = RWKV_HD
    nrow = hb * hd
    ones = jnp.ones((hd, hd), BF16)
    eye = (lax.broadcasted_iota(jnp.int32, (hd, hd), 0) == lax.broadcasted_iota(jnp.int32, (hd, hd), 1)).astype(F32)
    sub = min(tt, 8)

    def row_sums(x):
        hi = x.astype(BF16)
        lo = (x - hi.astype(F32)).astype(BF16)
        return (jnp.dot(hi, ones, preferred_element_type=F32) + jnp.dot(lo, ones, preferred_element_type=F32))

    def block(tb, carry):
        rows = pl.ds(pl.multiple_of(tb * sub, sub), sub)
        blk = [ref[0, rows, :] for ref in (kk_ref, w_ref, b_ref, k_ref, v_ref, r_ref)]
        blk_odd = [pltpu.roll(a, nrow - hd, axis=1) for a in blk] if hb > 1 else blk

        def vec(q, i, h):
            src, base = (blk[q], h * hd) if h % 2 == 0 else (blk_odd[q], (h - 1) * hd)
            return src[i:i + 1, base:base + hd]

        heads = range(hb)
        hs = [slice(h * hd, (h + 1) * hd) for h in heads]
        ys = [[] for _ in heads]
        for i in range(sub):
            s = [s_ref[0, hs[h], :] for h in heads]
            sa = -row_sums(jnp.concatenate([s[h] * vec(0, i, h) for h in heads], axis=0))
            vcol = jnp.dot(jnp.concatenate([eye * vec(4, i, h) for h in heads], axis=0).astype(BF16), ones,
                           preferred_element_type=F32)
            for h in heads:
                s[h] = s[h] * vec(1, i, h) + sa[hs[h]] * vec(2, i, h) + vcol[hs[h]] * vec(3, i, h)
                s_ref[0, hs[h], :] = s[h]
            ycol = jnp.dot(jnp.concatenate([s[h] * vec(5, i, h) for h in heads], axis=0).astype(BF16), ones,
                           preferred_element_type=F32)
            for h in heads:
                ys[h].append(jnp.sum(ycol[hs[h]] * eye, axis=0, keepdims=True))
        for h in heads:
            y_ref[0, rows, hs[h]] = jnp.concatenate(ys[h], axis=0)
        return carry

    lax.fori_loop(0, tt // sub, block, 0)


def _rwkv_scan(r, w, k, v, kk, b, s0):
    bsz, n, _ = r.shape
    hb = RWKV_HEAD_BLOCK
    tt = math.gcd(n, 256)
    cw = hb * RWKV_HD
    vec = pl.BlockSpec((1, tt, cw), lambda bi, hi, ti: (bi, ti, hi))
    st = pl.BlockSpec((1, cw, RWKV_HD), lambda bi, hi, ti: (bi, hi, 0))
    y, s_fin = pl.pallas_call(
        functools.partial(_rwkv_scan_kernel, tt=tt, hb=hb),
        grid=(bsz, RWKV_HEADS // hb, n // tt),
        in_specs=[vec] * 6 + [st],
        out_specs=[vec, st],
        out_shape=[jax.ShapeDtypeStruct((bsz, n, BR_W), F32),
                   jax.ShapeDtypeStruct((bsz, RWKV_HEADS * RWKV_HD, RWKV_HD), F32)],
        compiler_params=pltpu.CompilerParams(
            dimension_semantics=("arbitrary", "arbitrary", "arbitrary"),
            vmem_limit_bytes=VMEM_LIMIT_BYTES),
    )(r, w, k, v, kk, b, s0.reshape(bsz, RWKV_HEADS * RWKV_HD, RWKV_HD))
    return y, s_fin.reshape(bsz, RWKV_HEADS, RWKV_HD, RWKV_HD)


ATTN_TILE = 128


def _attn_kernel(*refs, mode, n_keys):
    if mode == "cmp":
        q_ref, k_ref, v_ref, o_ref, p_ref = refs
    elif mode == "sel":
        q_ref, k_ref, v_ref, sel_ref, o_ref = refs
    else:
        q_ref, k_ref, v_ref, o_ref = refs
    tq = tk = ATTN_TILE
    qi = pl.program_id(2)
    scale = HEAD_DIM ** -0.5
    q = q_ref[0]
    qs = jnp.concatenate([q[:, h * HEAD_DIM:(h + 1) * HEAD_DIM] for h in range(NSA_HG)], axis=0)
    mm_dt, mm_prec = (F32, lax.Precision.HIGHEST) if mode == "cmp" else (BF16, None)
    qs = qs.astype(mm_dt)
    tpos = qi * tq + lax.broadcasted_iota(jnp.int32, (NSA_HG * tq, tk), 0) % tq
    kiota = lax.broadcasted_iota(jnp.int32, (NSA_HG * tq, tk), 1)

    def scores(kj):
        k_t = k_ref[0, pl.ds(pl.multiple_of(kj * tk, tk), tk), :].astype(mm_dt)
        s = lax.dot_general(qs, k_t, (((1,), (1,)), ((), ())), precision=mm_prec,
                            preferred_element_type=F32) * scale
        return s, kiota + kj * tk

    def finish(acc, l):
        o = acc / jnp.maximum(l, TINY)
        o_ref[0] = jnp.concatenate([o[h * tq:(h + 1) * tq] for h in range(NSA_HG)], axis=1)

    if mode == "cmp":
        s, kidx = scores(0)
        mask = (kidx * CMP_STRIDE + (CMP_LEN - 1) <= tpos) & (kidx < n_keys)
        s = jnp.where(mask, s, NEG)
        p = jnp.where(mask, jnp.exp(s - jnp.max(s, axis=-1, keepdims=True)), 0.0)
        p = p / jnp.maximum(jnp.sum(p, axis=-1, keepdims=True), TINY)
        p_ref[0, 0] = sum(p[h * tq:(h + 1) * tq] for h in range(NSA_HG))
        o = jnp.dot(p.astype(BF16), v_ref[0].astype(BF16), preferred_element_type=F32)
        o_ref[0] = jnp.concatenate([o[h * tq:(h + 1) * tq] for h in range(NSA_HG)], axis=1)
        return

    if mode == "sel":
        sel = sel_ref[0, 0].astype(BF16)
        n_blk = sel.shape[1]
        blk_row = lax.broadcasted_iota(jnp.int32, (n_blk, tk), 0)
        blk_col = lax.broadcasted_iota(jnp.int32, (n_blk, tk), 1)
        lo = 0
    else:
        lo = jnp.maximum(qi - WINDOW // tk, 0)

    def body(kj, carry):
        m, l, acc = carry
        s, kpos = scores(kj)
        mask = kpos <= tpos
        if mode == "sel":
            expand = jnp.where((blk_col + kj * tk) // SEL_BLOCK == blk_row, 1.0, 0.0).astype(BF16)
            picked = jnp.dot(sel, expand, preferred_element_type=F32) > 0.5
            mask = mask & jnp.concatenate([picked] * NSA_HG, axis=0)
        else:
            mask = mask & (kpos >= tpos - WINDOW)
        s = jnp.where(mask, s, NEG)
        m_new = jnp.maximum(m, jnp.max(s, axis=-1, keepdims=True))
        alpha = jnp.exp(m - m_new)
        p = jnp.where(mask, jnp.exp(s - m_new), 0.0)
        l = alpha * l + jnp.sum(p, axis=-1, keepdims=True)
        v_t = v_ref[0, pl.ds(pl.multiple_of(kj * tk, tk), tk), :].astype(BF16)
        acc = alpha * acc + jnp.dot(p.astype(BF16), v_t, preferred_element_type=F32)
        return m_new, l, acc

    rows = NSA_HG * tq
    init = (jnp.full((rows, 1), NEG, F32), jnp.zeros((rows, 1), F32), jnp.zeros((rows, HEAD_DIM), F32))
    _, l, acc = lax.fori_loop(lo, qi + 1, body, init)
    finish(acc, l)


def _attn(mode, q, k_arr, k_lane_block, v_arr, v_lane_block, sel=None, n_keys=None):
    bsz, n, _ = q.shape
    length = k_arr.shape[1]
    tq = ATTN_TILE
    assert n % tq == 0 and length % ATTN_TILE == 0
    gw = NSA_HG * HEAD_DIM
    q_spec = pl.BlockSpec((1, tq, gw), lambda bi, gi, qi: (bi, qi, gi))
    k_spec = pl.BlockSpec((1, length, HEAD_DIM), lambda bi, gi, qi: (bi, 0, k_lane_block(gi)))
    v_spec = pl.BlockSpec((1, length, HEAD_DIM), lambda bi, gi, qi: (bi, 0, v_lane_block(gi)))
    in_specs = [q_spec, k_spec, v_spec]
    args = [q, k_arr, v_arr]
    out_specs = [q_spec]
    out_shape = [jax.ShapeDtypeStruct((bsz, n, NSA_HEADS * HEAD_DIM), F32)]
    if mode == "sel":
        n_blk = sel.shape[-1]
        in_specs.append(pl.BlockSpec((1, 1, tq, n_blk), lambda bi, gi, qi: (bi, gi, qi, 0)))
        args.append(sel)
    if mode == "cmp":
        out_specs.append(pl.BlockSpec((1, 1, tq, ATTN_TILE), lambda bi, gi, qi: (bi, gi, qi, 0)))
        out_shape.append(jax.ShapeDtypeStruct((bsz, NSA_KV, n, ATTN_TILE), F32))
    out = pl.pallas_call(
        functools.partial(_attn_kernel, mode=mode, n_keys=n_keys),
        grid=(bsz, NSA_KV, n // tq),
        in_specs=in_specs,
        out_specs=out_specs,
        out_shape=out_shape,
        compiler_params=pltpu.CompilerParams(
            dimension_semantics=("arbitrary", "arbitrary", "arbitrary"),
            vmem_limit_bytes=VMEM_LIMIT_BYTES),
    )(*args)
    return out if mode == "cmp" else out[0]


def _rms_norm(x, g):
    xf = x.astype(F32)
    y = xf * lax.rsqrt(jnp.mean(xf * xf, axis=-1, keepdims=True) + NORM_EPS)
    return (y * g.astype(F32)).astype(x.dtype)


def _masked_softmax(s, mask):
    s = jnp.where(mask, s.astype(F32), NEG)
    p = jnp.where(mask, jnp.exp(s - jnp.max(s, axis=-1, keepdims=True)), 0.0)
    return p / jnp.maximum(jnp.sum(p, axis=-1, keepdims=True), TINY)


def _partial_rope(x, pos):
    half = ROT_DIM // 2
    inv_freq = jnp.power(ROPE_THETA, -jnp.arange(half, dtype=F32) / half)
    ang = pos.astype(F32)[:, None] * inv_freq[None, :]
    cos = jnp.cos(ang)[:, None, :]
    sin = jnp.sin(ang)[:, None, :]
    xr = x[..., :ROT_DIM].astype(F32)
    x1, x2 = xr[..., :half], xr[..., half:]
    rot = jnp.concatenate([x1 * cos - x2 * sin, x2 * cos + x1 * sin], axis=-1)
    return jnp.concatenate([rot.astype(x.dtype), x[..., ROT_DIM:]], axis=-1)


def _causal_dwconv(x, buf, w, b):
    width = w.shape[0]
    n = x.shape[1]
    xx = jnp.concatenate([buf.astype(x.dtype), x], axis=1)
    y = b + sum(xx[:, j:j + n] * w[j] for j in range(width))
    return y, xx[:, xx.shape[1] - (width - 1):]


def _linear_combine(e1, e2):
    a1, b1 = e1
    a2, b2 = e2
    return a1 * a2, a2 * b1 + b2


def _rglru_mixer(x_in, gate_in, conv_buf, h0, lp):
    b, n, _ = x_in.shape
    xc, conv_new = _causal_dwconv(x_in, conv_buf, lp["lru_conv_w"], lp["lru_conv_b"])
    xh = xc.reshape(b, n, LRU_HEADS, LRU_HD)
    r = jax.nn.sigmoid((jnp.einsum("bnhi,hij->bnhj", xh, lp["lru_wa"]).reshape(b, n, BR_W)
                        + lp["lru_ba"]).astype(F32))
    i = jax.nn.sigmoid((jnp.einsum("bnhi,hij->bnhj", xh, lp["lru_wx"]).reshape(b, n, BR_W)
                        + lp["lru_bx"]).astype(F32))
    log_a = -LRU_C * r * jax.nn.softplus(-lp["lru_lambda"].astype(F32))
    a = jnp.exp(log_a)
    u = jnp.sqrt(-jnp.expm1(2.0 * log_a)) * (i * xc.astype(F32))
    u = u.at[:, 0].add(a[:, 0] * h0.astype(F32))
    _, h = lax.associative_scan(_linear_combine, (a, u), axis=1)
    y = (h * jax.nn.gelu(gate_in.astype(F32))).astype(x_in.dtype)
    return y, conv_new, h[:, -1].astype(x_in.dtype)


def _compress_blocks(rows, pe, w1, w2):
    b, length, g, d = rows.shape
    chunks = rows.reshape(b, length // CMP_STRIDE, CMP_STRIDE, g, d)
    w1r = w1.reshape(CMP_LEN, d, CMP_HIDDEN)
    first = jnp.einsum("bcpgd,pde->bcge", chunks, w1r[:CMP_STRIDE])
    second = jnp.einsum("bcpgd,pde->bcge", chunks, w1r[CMP_STRIDE:])
    pe_term = jnp.einsum("pd,pde->e", pe, w1r)
    hidden = jax.nn.gelu(first[:, :-1] + second[:, 1:] + pe_term)
    return jnp.einsum("bcge,ed->bcgd", hidden, w2)


def _nsa_mixer(q, rows, win_kv, gates, q_off, lp):
    b, n = q.shape[0], q.shape[1]
    lpad = rows.shape[1]
    scale = HEAD_DIM ** -0.5
    tq = q_off + jnp.arange(n)
    q5 = q.reshape(b, n, NSA_KV, NSA_HG, HEAD_DIM)
    k_cmp = _compress_blocks(rows[:, :, 0], lp["cmp_pe"][0], lp["cmp_w1"][0], lp["cmp_w2"][0])
    v_cmp = _compress_blocks(rows[:, :, 1], lp["cmp_pe"][1], lp["cmp_w1"][1], lp["cmp_w2"][1])
    n_cmp = k_cmp.shape[1]
    cmp_end = jnp.arange(n_cmp) * CMP_STRIDE + (CMP_LEN - 1)
    k_cmp = _partial_rope(_rms_norm(k_cmp, lp["qk_norm"][1]), cmp_end)
    s_c = jnp.einsum("bqghd,bkgd->bghqk", q5, k_cmp) * scale
    p_c = _masked_softmax(s_c, cmp_end[None, :] <= tq[:, None])
    o_c = jnp.einsum("bghqk,bkgd->bqghd", p_c.astype(q.dtype), v_cmp)
    n_slc = lpad // SEL_BLOCK
    ratio = SEL_BLOCK // CMP_STRIDE
    imp = jnp.pad(p_c.sum(axis=2), ((0, 0), (0, 0), (0, 0), (1, 1)))
    imp_slc = (imp[..., :ratio * n_slc].reshape(b, NSA_KV, n, n_slc, ratio).sum(-1)
               + imp[..., ratio::ratio])
    blk = jnp.arange(n_slc)[None, :]
    cur = (tq // SEL_BLOCK)[:, None]
    valid = blk * SEL_BLOCK <= tq[:, None]
    forced = (blk == 0) | (blk == cur) | (blk == cur - 1)
    score = jnp.where(valid, imp_slc + jnp.where(forced, FORCE_BONUS, 0.0), NEG)
    k_sel = min(N_SEL, n_slc)
    top_val, top_idx = lax.top_k(score, k_sel)
    top_ok = top_val > 0.5 * NEG
    ks_blk = rows[:, :, 2].reshape(b, n_slc, SEL_BLOCK, NSA_KV, HEAD_DIM).transpose(0, 3, 1, 2, 4)
    vs_blk = rows[:, :, 3].reshape(b, n_slc, SEL_BLOCK, NSA_KV, HEAD_DIM).transpose(0, 3, 1, 2, 4)
    ks_blk = ks_blk.reshape(b * NSA_KV, n_slc, SEL_BLOCK, HEAD_DIM)
    vs_blk = vs_blk.reshape(b * NSA_KV, n_slc, SEL_BLOCK, HEAD_DIM)
    qc = math.gcd(n, SEL_Q_CHUNK)
    nc = n // qc
    m = k_sel * SEL_BLOCK

    def sel_chunk(args):
        q_c, idx_c, ok_c, t_c = args
        flat = idx_c.reshape(b * NSA_KV, qc * k_sel)
        kg = jax.vmap(lambda blocks, ix: blocks[ix])(ks_blk, flat).reshape(b, NSA_KV, qc, m, HEAD_DIM)
        vg = jax.vmap(lambda blocks, ix: blocks[ix])(vs_blk, flat).reshape(b, NSA_KV, qc, m, HEAD_DIM)
        s = jnp.einsum("bqghd,bgqmd->bghqm", q_c, kg) * scale
        kpos = (idx_c[..., None] * SEL_BLOCK + jnp.arange(SEL_BLOCK)).reshape(b, NSA_KV, qc, m)
        mask = (kpos <= t_c[None, None, :, None]) & jnp.repeat(ok_c, SEL_BLOCK, axis=-1)
        p = _masked_softmax(s, mask[:, :, None])
        return jnp.einsum("bghqm,bgqmd->bqghd", p.astype(q_c.dtype), vg)

    chunks = (jnp.moveaxis(q5.reshape(b, nc, qc, NSA_KV, NSA_HG, HEAD_DIM), 1, 0),
              jnp.moveaxis(top_idx.reshape(b, NSA_KV, nc, qc, k_sel), 2, 0),
              jnp.moveaxis(top_ok.reshape(b, NSA_KV, nc, qc, k_sel), 2, 0),
              tq.reshape(nc, qc))
    o_s = jnp.moveaxis(lax.map(sel_chunk, chunks), 0, 1).reshape(b, n, NSA_KV, NSA_HG, HEAD_DIM)
    qb = math.gcd(n, WIN_Q_BLOCK)
    nb = n // qb
    kw, vw = win_kv[:, :, 0], win_kv[:, :, 1]

    def win_block(i):
        start = i * qb
        q_b = lax.dynamic_slice_in_dim(q5, start, qb, axis=1)
        k_b = lax.dynamic_slice_in_dim(kw, start, qb + WINDOW, axis=1)
        v_b = lax.dynamic_slice_in_dim(vw, start, qb + WINDOW, axis=1)
        t_b = q_off + start + jnp.arange(qb)
        kpos = q_off - WINDOW + start + jnp.arange(qb + WINDOW)
        mask = ((kpos[None, :] <= t_b[:, None]) & (kpos[None, :] >= t_b[:, None] - WINDOW)
                & (kpos[None, :] >= 0))
        s = jnp.einsum("bqghd,bkgd->bghqk", q_b, k_b) * scale
        p = _masked_softmax(s, mask)
        return jnp.einsum("bghqk,bkgd->bqghd", p.astype(q_b.dtype), v_b)

    o_w = jnp.moveaxis(lax.map(win_block, jnp.arange(nb)), 0, 1).reshape(b, n, NSA_KV, NSA_HG, HEAD_DIM)
    g5 = gates.reshape(b, n, NSA_KV, NSA_HG, 3).astype(q.dtype)
    o = g5[..., 0:1] * o_c + g5[..., 1:2] * o_s + g5[..., 2:3] * o_w
    return o.reshape(b, n, NSA_HEADS * HEAD_DIM)


def _nsa_mixer_prompt(q, nsa_rows, win_rows, gates, lp):
    b, n = q.shape[0], q.shape[1]
    tq = jnp.arange(n)
    q2 = q.reshape(b, n, NSA_HEADS * HEAD_DIM)
    rows2 = nsa_rows.reshape(b, n, 4 * NSA_KV * HEAD_DIM)
    win2 = win_rows.reshape(b, n, 2 * NSA_KV * HEAD_DIM)
    k_cmp = _compress_blocks(nsa_rows[:, :, 0], lp["cmp_pe"][0], lp["cmp_w1"][0], lp["cmp_w2"][0])
    v_cmp = _compress_blocks(nsa_rows[:, :, 1], lp["cmp_pe"][1], lp["cmp_w1"][1], lp["cmp_w2"][1])
    n_cmp = k_cmp.shape[1]
    assert n_cmp <= ATTN_TILE
    cmp_end = jnp.arange(n_cmp) * CMP_STRIDE + (CMP_LEN - 1)
    k_cmp = _partial_rope(_rms_norm(k_cmp, lp["qk_norm"][1]), cmp_end)
    pad = ((0, 0), (0, ATTN_TILE - n_cmp), (0, 0))
    k_cmp2 = jnp.pad(k_cmp.reshape(b, n_cmp, NSA_KV * HEAD_DIM), pad)
    v_cmp2 = jnp.pad(v_cmp.reshape(b, n_cmp, NSA_KV * HEAD_DIM), pad)
    o_c, psum = _attn("cmp", q2, k_cmp2, lambda g: g, v_cmp2, lambda g: g, n_keys=n_cmp)
    n_slc = n // SEL_BLOCK
    ratio = SEL_BLOCK // CMP_STRIDE
    imp = jnp.pad(psum[..., :n_cmp], ((0, 0), (0, 0), (0, 0), (1, 1)))
    imp_slc = (imp[..., :ratio * n_slc].reshape(b, NSA_KV, n, n_slc, ratio).sum(-1)
               + imp[..., ratio::ratio])
    blk = jnp.arange(n_slc)[None, :]
    cur = (tq // SEL_BLOCK)[:, None]
    valid = blk * SEL_BLOCK <= tq[:, None]
    forced = (blk == 0) | (blk == cur) | (blk == cur - 1)
    score = jnp.where(valid, imp_slc + jnp.where(forced, FORCE_BONUS, 0.0), NEG)
    k_sel = min(N_SEL, n_slc)
    top_val, top_idx = lax.top_k(score, k_sel)
    top_ok = top_val > 0.5 * NEG
    sel = jnp.any((top_idx[..., None] == jnp.arange(n_slc)) & top_ok[..., None], axis=-2).astype(F32)
    o_s = _attn("sel", q2, rows2, lambda g: 2 * NSA_KV + g, rows2, lambda g: 3 * NSA_KV + g, sel=sel)
    o_w = _attn("win", q2, win2, lambda g: g, win2, lambda g: NSA_KV + g)
    g4 = gates.astype(q.dtype)
    o4 = lambda o: o.reshape(b, n, NSA_HEADS, HEAD_DIM)
    o = g4[..., 0:1] * o4(o_c) + g4[..., 1:2] * o4(o_s) + g4[..., 2:3] * o4(o_w)
    return o.reshape(b, n, NSA_HEADS * HEAD_DIM)


def _rwkv7_mixer(z, shift_buf, s0, lp):
    b, n, _ = z.shape
    prev = jnp.concatenate([shift_buf[:, None].astype(z.dtype), z[:, :-1]], axis=1)
    zm = z + (prev - z) * lp["rwkv_mu"]
    r, k, v, wd, ad, gd = jnp.split(zm, RWKV_OFFSETS, axis=-1)
    w_log = -jax.nn.softplus(-(lp["rwkv_w0"] + jnp.tanh(wd) @ lp["rwkv_w2"]).astype(F32)) - 0.5
    decay = jnp.exp(-jnp.exp(w_log))
    a = jax.nn.sigmoid((lp["rwkv_a0"] + ad @ lp["rwkv_a2"]).astype(F32))
    g = jax.nn.sigmoid(gd) @ lp["rwkv_g2"]
    hs = lambda t: t.astype(F32).reshape(b, n, RWKV_HEADS, RWKV_HD)
    kk = hs(k * lp["rwkv_kk"])
    kk = kk / jnp.maximum(jnp.sqrt(jnp.sum(kk * kk, axis=-1, keepdims=True)), 1e-12)
    k_h = hs(k.astype(F32) * (1.0 + (a - 1.0) * lp["rwkv_ka"].astype(F32)))
    r_h, v_h, a_h = hs(r), hs(v), hs(a)
    flat = lambda t: t.reshape(b, n, BR_W)
    y, s_fin = _rwkv_scan(r.astype(F32), decay, flat(k_h), v.astype(F32), flat(kk), flat(kk * a_h),
                          s0.astype(F32))
    y = y.reshape(b, n, RWKV_HEADS, RWKV_HD)
    mu = jnp.mean(y, axis=-1, keepdims=True)
    var = jnp.mean((y - mu) ** 2, axis=-1, keepdims=True)
    y = ((y - mu) * lax.rsqrt(var + GN_EPS)).reshape(b, n, BR_W) * lp["rwkv_gn_g"] + lp["rwkv_gn_b"]
    bonus = jnp.sum(r_h * k_h * lp["rwkv_rk"].astype(F32), axis=-1, keepdims=True) * v_h
    y = y + bonus.reshape(b, n, BR_W)
    out = (y * g.astype(F32)).astype(z.dtype)
    return out, s_fin.astype(z.dtype), z[:, -1]


def _trunk_layer(x, lp, st, q_off, win_keep):
    b, n, _ = x.shape
    dt = x.dtype
    t = b * n
    xn = _rms_norm(x, lp["norm1"]).reshape(t, D_MODEL).astype(BF16)
    (lru_x, lru_gate, q, kv_cmp, kv_sel, kv_win, nsa_gate, rwkv_in,
     merge_gate) = [_proj(xn, w).reshape(b, n, w.shape[1]) for w in lp["w_in"]]
    y_a, lru_conv_new, lru_h_new = _rglru_mixer(lru_x, lru_gate, st["lru_conv"], st["lru_h"], lp)
    pos = q_off + jnp.arange(n)
    qn = lp["qk_norm"]
    q = _partial_rope(_rms_norm(q.reshape(b, n, NSA_HEADS, HEAD_DIM), qn[0]), pos)
    kv_cmp = kv_cmp.reshape(b, n, 2, NSA_KV, HEAD_DIM)
    kv_sel = kv_sel.reshape(b, n, 2, NSA_KV, HEAD_DIM)
    kv_win = kv_win.reshape(b, n, 2, NSA_KV, HEAD_DIM)
    k_sel = _partial_rope(_rms_norm(kv_sel[:, :, 0], qn[2]), pos)
    k_win = _partial_rope(_rms_norm(kv_win[:, :, 0], qn[3]), pos)
    nsa_rows = jnp.stack([kv_cmp[:, :, 0], kv_cmp[:, :, 1], k_sel, kv_sel[:, :, 1]], axis=2)
    win_rows = jnp.stack([k_win, kv_win[:, :, 1]], axis=2)
    total = q_off + n
    padded = -(-total // SEL_BLOCK) * SEL_BLOCK
    win_all = jnp.concatenate([st["win"].astype(dt), win_rows], axis=1)
    win_new = win_all[:, win_all.shape[1] - win_keep:]
    nsa_gates = jax.nn.sigmoid(nsa_gate.reshape(b, n, NSA_HEADS, 3))
    if q_off == 0 and st["nsa_past"].shape[1] == 0 and st["win"].shape[1] == 0 and n % ATTN_TILE == 0:
        y_b = _nsa_mixer_prompt(q, nsa_rows, win_rows, nsa_gates, lp)
    else:
        rows_all = jnp.concatenate([st["nsa_past"].astype(dt), nsa_rows,
                                    jnp.zeros((b, padded - total, 4, NSA_KV, HEAD_DIM), dt)], axis=1)
        win_full = jnp.pad(win_all, ((0, 0), (WINDOW + n - win_all.shape[1], 0), (0, 0), (0, 0), (0, 0)))
        y_b = _nsa_mixer(q, rows_all, win_full, nsa_gates, q_off, lp)
    y_c, rwkv_s_new, rwkv_shift_new = _rwkv7_mixer(rwkv_in, st["rwkv_shift"], st["rwkv_s"], lp)
    gate = jax.nn.sigmoid(merge_gate.reshape(b, n, 3, D_MODEL))
    mix = 0.0
    for bi, y in enumerate((y_a, y_b, y_c)):
        pb = _proj(y.reshape(t, BR_W), lp["w_branch"][bi]).reshape(b, n, D_MODEL)
        mix = mix + gate[:, :, bi] * pb
    x = x + _proj(mix.reshape(t, D_MODEL), lp["w_out"]).reshape(b, n, D_MODEL)
    xn2 = _rms_norm(x, lp["norm2"]).reshape(t, D_MODEL).astype(BF16)
    hg, hv = [_proj(xn2, w).reshape(b, n, D_FF) for w in lp["ffn_w_in"]]
    hc, ffn_conv_new = _causal_dwconv(hg, st["ffn_conv"], lp["ffn_conv_w"], lp["ffn_conv_b"])
    act = jax.nn.gelu(hc) * hv
    x = x + _proj(act.reshape(t, D_FF), lp["ffn_w_down"]).reshape(b, n, D_MODEL)
    new_state = {"nsa": nsa_rows, "win": win_new, "lru_h": lru_h_new.astype(dt),
                 "lru_conv": lru_conv_new.astype(dt), "rwkv_s": rwkv_s_new.astype(dt),
                 "rwkv_shift": rwkv_shift_new.astype(dt), "ffn_conv": ffn_conv_new.astype(dt)}
    return x, new_state


def kernel(x_prompt, x_sample, cache_nsa_kv, state_win_kv, state_lru_h, state_lru_conv,
           state_rwkv_s, state_rwkv_shift, state_ffn_conv, page_table,
           norm1_g, norm2_g, w_in, lru_conv_w, lru_conv_b, lru_wa, lru_ba, lru_wx, lru_bx,
           lru_lambda, qk_norm_g, cmp_pe, cmp_w1, cmp_w2, rwkv_mu, rwkv_w0, rwkv_w2,
           rwkv_a0, rwkv_a2, rwkv_g2, rwkv_kk, rwkv_ka, rwkv_rk, rwkv_gn_g, rwkv_gn_b,
           w_branch, w_out, ffn_w_in, ffn_conv_w, ffn_conv_b, ffn_w_down):
    dt = x_prompt.dtype
    bp, n_p, _ = x_prompt.shape
    bs = x_sample.shape[0]
    depth = w_in.shape[0]
    past_len = page_table.shape[1] * PAGE_SIZE
    xp, xs = x_prompt, x_sample
    new_p = {name: [] for name in STATE_NAMES}
    new_s = {name: [] for name in STATE_NAMES}
    for l in range(depth):
        in_bounds = (0,) + IN_OFFSETS + (D_IN,)
        w_in_segs = [w_in[l][:, a:z].astype(BF16) for a, z in zip(in_bounds[:-1], in_bounds[1:])]
        lp = {"norm1": norm1_g[l], "norm2": norm2_g[l], "w_in": w_in_segs,
              "lru_conv_w": lru_conv_w[l], "lru_conv_b": lru_conv_b[l], "lru_wa": lru_wa[l],
              "lru_ba": lru_ba[l], "lru_wx": lru_wx[l], "lru_bx": lru_bx[l],
              "lru_lambda": lru_lambda[l], "qk_norm": qk_norm_g[l], "cmp_pe": cmp_pe[l],
              "cmp_w1": cmp_w1[l], "cmp_w2": cmp_w2[l], "rwkv_mu": rwkv_mu[l],
              "rwkv_w0": rwkv_w0[l], "rwkv_w2": rwkv_w2[l], "rwkv_a0": rwkv_a0[l],
              "rwkv_a2": rwkv_a2[l], "rwkv_g2": rwkv_g2[l], "rwkv_kk": rwkv_kk[l],
              "rwkv_ka": rwkv_ka[l], "rwkv_rk": rwkv_rk[l], "rwkv_gn_g": rwkv_gn_g[l],
              "rwkv_gn_b": rwkv_gn_b[l], "w_branch": w_branch[l].astype(BF16),
              "w_out": w_out[l].astype(BF16), "ffn_w_in": [ffn_w_in[l][:, :D_FF].astype(BF16), ffn_w_in[l][:, D_FF:].astype(BF16)],
              "ffn_conv_w": ffn_conv_w[l], "ffn_conv_b": ffn_conv_b[l],
              "ffn_w_down": ffn_w_down[l].astype(BF16)}
        st_p = {"nsa_past": jnp.zeros((bp, 0, 4, NSA_KV, HEAD_DIM), dt),
                "win": jnp.zeros((bp, 0, 2, NSA_KV, HEAD_DIM), dt),
                "lru_h": jnp.zeros((bp, BR_W), dt),
                "lru_conv": jnp.zeros((bp, LRU_CONV - 1, BR_W), dt),
                "rwkv_s": jnp.zeros((bp, RWKV_HEADS, RWKV_HD, RWKV_HD), dt),
                "rwkv_shift": jnp.zeros((bp, RWKV_IN), dt),
                "ffn_conv": jnp.zeros((bp, FFN_CONV - 1, D_FF), dt)}
        xp, out_p = _trunk_layer(xp, lp, st_p, 0, min(WINDOW, n_p))
        past = cache_nsa_kv[l, page_table].reshape(bs, past_len, 4, NSA_KV, HEAD_DIM)
        st_s = {"nsa_past": past, "win": state_win_kv[l], "lru_h": state_lru_h[l],
                "lru_conv": state_lru_conv[l], "rwkv_s": state_rwkv_s[l],
                "rwkv_shift": state_rwkv_shift[l], "ffn_conv": state_ffn_conv[l]}
        xs, out_s = _trunk_layer(xs, lp, st_s, past_len, state_win_kv.shape[2])
        for name in STATE_NAMES:
            new_p[name].append(out_p[name])
            new_s[name].append(out_s[name])
    sp = {name: jnp.stack(new_p[name], axis=0) for name in STATE_NAMES}
    ss = {name: jnp.stack(new_s[name], axis=0) for name in STATE_NAMES}
    return (xp, xs, sp["nsa"], sp["win"], sp["lru_h"], sp["lru_conv"], sp["rwkv_s"],
            sp["rwkv_shift"], sp["ffn_conv"], ss["nsa"], ss["win"], ss["lru_h"], ss["lru_conv"],
            ss["rwkv_s"], ss["rwkv_shift"], ss["ffn_conv"])
```

```python
import functools
import math

import jax
import jax.numpy as jnp
import numpy as np
from jax import lax
from jax.experimental import pallas as pl
from jax.experimental.pallas import tpu as pltpu

F32 = jnp.float32
BF16 = jnp.bfloat16

D_MODEL = 4096
PAGE_SIZE = 128
BR_W = D_MODEL // 2
LRU_HEADS = 16
LRU_HD = BR_W // LRU_HEADS
LRU_CONV = 4
LRU_C = 8.0
HEAD_DIM = 128
NSA_HEADS = BR_W // HEAD_DIM
NSA_KV = 4
NSA_HG = NSA_HEADS // NSA_KV
ROT_DIM = HEAD_DIM // 4
ROPE_THETA = 500000.0
CMP_LEN = 32
CMP_STRIDE = 16
CMP_HIDDEN = 2 * HEAD_DIM
SEL_BLOCK = 64
N_SEL = 16
WINDOW = 512
FORCE_BONUS = 1e4
NEG = -1e30
TINY = 1e-30
RWKV_HD = 64
RWKV_HEADS = BR_W // RWKV_HD
LORA_W = 64
LORA_A = 64
LORA_G = 128
RWKV_SIZES = (BR_W, BR_W, BR_W, LORA_W, LORA_A, LORA_G)
RWKV_IN = sum(RWKV_SIZES)
RWKV_OFFSETS = tuple(int(v) for v in np.cumsum(RWKV_SIZES)[:-1])
GN_EPS = 64e-5
D_FF = ((8 * D_MODEL // 3 + 255) // 256) * 256
FFN_CONV = 3
NORM_EPS = 1e-6
IN_SIZES = (BR_W, BR_W, NSA_HEADS * HEAD_DIM, 2 * NSA_KV * HEAD_DIM, 2 * NSA_KV * HEAD_DIM,
            2 * NSA_KV * HEAD_DIM, 3 * NSA_HEADS, RWKV_IN, 3 * D_MODEL)
D_IN = sum(IN_SIZES)
IN_OFFSETS = tuple(int(v) for v in np.cumsum(IN_SIZES)[:-1])
STATE_NAMES = ("nsa", "win", "lru_h", "lru_conv", "rwkv_s", "rwkv_shift", "ffn_conv")

VMEM_LIMIT_BYTES = 56 * 1024 * 1024


def _mm_kernel(x_ref, w_ref, o_ref, *, nk):
    part = jnp.dot(x_ref[...], w_ref[...], preferred_element_type=F32)
    if nk == 1:
        o_ref[...] = part
    else:
        k = pl.program_id(2)

        @pl.when(k == 0)
        def _():
            o_ref[...] = part

        @pl.when(k != 0)
        def _():
            o_ref[...] += part


def _matmul(x, w, *, tm, tn, tk=None):
    m, kdim = x.shape
    n = w.shape[1]
    tk = kdim if tk is None else tk
    assert m % tm == 0 and kdim % tk == 0
    nk = kdim // tk
    grid = (m // tm, pl.cdiv(n, tn), nk)
    return pl.pallas_call(
        functools.partial(_mm_kernel, nk=nk),
        grid=grid,
        in_specs=[pl.BlockSpec((tm, tk), lambda i, j, k: (i, k)),
                  pl.BlockSpec((tk, tn), lambda i, j, k: (k, j))],
        out_specs=pl.BlockSpec((tm, tn), lambda i, j, k: (i, j)),
        out_shape=jax.ShapeDtypeStruct((m, n), F32),
        compiler_params=pltpu.CompilerParams(
            dimension_semantics=("arbitrary", "arbitrary", "arbitrary"),
            vmem_limit_bytes=VMEM_LIMIT_BYTES),
    )(x, w)


def _proj(x2d, w_bf16):
    m, kdim = x2d.shape
    n = w_bf16.shape[1]
    xb = x2d.astype(BF16)
    tm, tn = (1024, 512) if m >= 1024 else (m, 1024)
    tn = min(tn, n)
    tk = kdim if kdim <= 4096 else kdim // 2
    return _matmul(xb, w_bf16, tm=tm, tn=tn, tk=tk)


RWKV_HEAD_BLOCK = 16


def _rwkv_scan_kernel(r_ref, w_ref, k_ref, v_ref, kk_ref, b_ref, s0_ref, y_ref, s_ref, *, tt, hb):
    @pl.when(pl.program_id(2) == 0)
    def _():
        s_ref[...] = s0_ref[...]

    hd = RWKV_HD
    nrow = hb * hd
    ones = jnp.ones((hd, hd), BF16)
    eye = (lax.broadcasted_iota(jnp.int32, (hd, hd), 0) == lax.broadcasted_iota(jnp.int32, (hd, hd), 1)).astype(F32)
    sub = min(tt, 8)

    def row_sums(x):
        hi = x.astype(BF16)
        lo = (x - hi.astype(F32)).astype(BF16)
        return (jnp.dot(hi, ones, preferred_element_type=F32) + jnp.dot(lo, ones, preferred_element_type=F32))

    def block(tb, carry):
        rows = pl.ds(pl.multiple_of(tb * sub, sub), sub)
        blk = [ref[0, rows, :] for ref in (kk_ref, w_ref, b_ref, k_ref, v_ref, r_ref)]
        blk_odd = [pltpu.roll(a, nrow - hd, axis=1) for a in blk] if hb > 1 else blk

        def vec(q, i, h):
            src, base = (blk[q], h * hd) if h % 2 == 0 else (blk_odd[q], (h - 1) * hd)
            return src[i:i + 1, base:base + hd]

        heads = range(hb)
        hs = [slice(h * hd, (h + 1) * hd) for h in heads]
        ys = [[] for _ in heads]
        for i in range(sub):
            s = [s_ref[0, hs[h], :] for h in heads]
            sa = row_sums(jnp.concatenate([s[h] * vec(0, i, h) for h in heads], axis=0))
            vcol = jnp.dot(jnp.concatenate([eye * vec(4, i, h) for h in heads], axis=0).astype(BF16), ones,
                           preferred_element_type=F32)
            for h in heads:
                s[h] = s[h] * vec(1, i, h) + sa[hs[h]] * vec(2, i, h) + vcol[hs[h]] * vec(3, i, h)
                s_ref[0, hs[h], :] = s[h]
            ycol = jnp.dot(jnp.concatenate([s[h] * vec(5, i, h) for h in heads], axis=0).astype(BF16), ones,
                           preferred_element_type=F32)
            for h in heads:
                ys[h].append(jnp.sum(ycol[hs[h]] * eye, axis=0, keepdims=True))
        for h in heads:
            y_ref[0, rows, hs[h]] = jnp.concatenate(ys[h], axis=0)
        return carry

    lax.fori_loop(0, tt // sub, block, 0)


def _rwkv_scan(r, w, k, v, kk, b, s0):
    bsz, n, _ = r.shape
    hb = RWKV_HEAD_BLOCK
    tt = math.gcd(n, 256)
    cw = hb * RWKV_HD
    vec = pl.BlockSpec((1, tt, cw), lambda bi, hi, ti: (bi, ti, hi))
    st = pl.BlockSpec((1, cw, RWKV_HD), lambda bi, hi, ti: (bi, hi, 0))
    y, s_fin = pl.pallas_call(
        functools.partial(_rwkv_scan_kernel, tt=tt, hb=hb),
        grid=(bsz, RWKV_HEADS // hb, n // tt),
        in_specs=[vec] * 6 + [st],
        out_specs=[vec, st],
        out_shape=[jax.ShapeDtypeStruct((bsz, n, BR_W), F32),
                   jax.ShapeDtypeStruct((bsz, RWKV_HEADS * RWKV_HD, RWKV_HD), F32)],
        compiler_params=pltpu.CompilerParams(
            dimension_semantics=("arbitrary", "arbitrary", "arbitrary"),
            vmem_limit_bytes=VMEM_LIMIT_BYTES),
    )(r, w, k, v, kk, b, s0.reshape(bsz, RWKV_HEADS * RWKV_HD, RWKV_HD))
    return y, s_fin.reshape(bsz, RWKV_HEADS, RWKV_HD, RWKV_HD)


def _lru_scan_kernel(a_ref, u_ref, h0_ref, h_ref, carry_ref, *, tt):
    @pl.when(pl.program_id(1) == 0)
    def _():
        carry_ref[...] = h0_ref[0]

    sub = min(tt, 8)

    def block(tb, h):
        rows = pl.ds(pl.multiple_of(tb * sub, sub), sub)
        a8, u8 = a_ref[0, rows, :], u_ref[0, rows, :]
        outs = []
        for i in range(sub):
            h = a8[i:i + 1] * h + u8[i:i + 1]
            outs.append(h)
        h_ref[0, rows, :] = jnp.concatenate(outs, axis=0)
        return h

    carry_ref[...] = lax.fori_loop(0, tt // sub, block, carry_ref[...])


def _lru_scan(a, u, h0):
    bsz, n, c = a.shape
    tt = math.gcd(n, 256)
    vec = pl.BlockSpec((1, tt, c), lambda bi, ti: (bi, ti, 0))
    return pl.pallas_call(
        functools.partial(_lru_scan_kernel, tt=tt),
        grid=(bsz, n // tt),
        in_specs=[vec, vec, pl.BlockSpec((1, 1, c), lambda bi, ti: (bi, 0, 0))],
        out_specs=vec,
        out_shape=jax.ShapeDtypeStruct((bsz, n, c), F32),
        scratch_shapes=[pltpu.VMEM((1, c), F32)],
        compiler_params=pltpu.CompilerParams(
            dimension_semantics=("arbitrary", "arbitrary"),
            vmem_limit_bytes=VMEM_LIMIT_BYTES),
    )(a, u, h0.reshape(bsz, 1, c))


CMP_PAGES_PER_STEP = 8
ROW_COMBOS = 4 * NSA_KV


def _compress_pages_kernel(pt_ref, *refs, pp):
    del pt_ref
    pages, (w1_ref, first_ref, second_ref) = refs[:pp], refs[pp:]
    cpp = PAGE_SIZE // CMP_STRIDE
    for kind in range(2):
        acc_f = jnp.zeros((NSA_KV * pp * cpp, CMP_HIDDEN), F32)
        acc_s = jnp.zeros((NSA_KV * pp * cpp, CMP_HIDDEN), F32)
        for p in range(CMP_STRIDE):
            x = jnp.concatenate(
                [page[pl.ds(p * ROW_COMBOS + kind * NSA_KV + g, cpp, stride=CMP_STRIDE * ROW_COMBOS), :]
                 for g in range(NSA_KV) for page in pages], axis=0).astype(BF16)
            acc_f += jnp.dot(x, w1_ref[kind, p], preferred_element_type=F32)
            acc_s += jnp.dot(x, w1_ref[kind, CMP_STRIDE + p], preferred_element_type=F32)
        for g in range(NSA_KV):
            first_ref[0, kind, g] = acc_f[g * pp * cpp:(g + 1) * pp * cpp]
            second_ref[0, kind, g] = acc_s[g * pp * cpp:(g + 1) * pp * cpp]


def _compress_pages(cache, layer, page_table, w1):
    depth, n_pool = cache.shape[0], cache.shape[1]
    bsz, n_pages = page_table.shape
    pp = math.gcd(n_pages, CMP_PAGES_PER_STEP)
    cpp = PAGE_SIZE // CMP_STRIDE
    cache2 = cache.reshape(depth, n_pool, PAGE_SIZE * ROW_COMBOS, HEAD_DIM)
    w1r = w1.reshape(2, CMP_LEN, HEAD_DIM, CMP_HIDDEN).astype(BF16)

    def page_spec(k):
        return pl.BlockSpec((None, None, PAGE_SIZE * ROW_COMBOS, HEAD_DIM),
                            lambda bi, pi, pt: (layer, pt[bi, pi * pp + k], 0, 0))

    out_spec = pl.BlockSpec((1, 2, NSA_KV, pp * cpp, CMP_HIDDEN), lambda bi, pi, pt: (bi, 0, 0, pi, 0))
    out_sds = jax.ShapeDtypeStruct((bsz, 2, NSA_KV, n_pages * cpp, CMP_HIDDEN), F32)
    return pl.pallas_call(
        functools.partial(_compress_pages_kernel, pp=pp),
        grid_spec=pltpu.PrefetchScalarGridSpec(
            num_scalar_prefetch=1,
            grid=(bsz, n_pages // pp),
            in_specs=[page_spec(k) for k in range(pp)]
            + [pl.BlockSpec((2, CMP_LEN, HEAD_DIM, CMP_HIDDEN), lambda bi, pi, pt: (0, 0, 0, 0))],
            out_specs=[out_spec, out_spec]),
        out_shape=[out_sds, out_sds],
        compiler_params=pltpu.CompilerParams(
            dimension_semantics=("arbitrary", "arbitrary"),
            vmem_limit_bytes=VMEM_LIMIT_BYTES),
    )(page_table, *([cache2] * pp), w1r)


ATTN_TILE = 128


def _attn_kernel(*refs, mode, n_keys):
    if mode == "cmp":
        q_ref, k_ref, v_ref, o_ref, p_ref = refs
    elif mode == "sel":
        q_ref, k_ref, v_ref, sel_ref, o_ref = refs
    else:
        q_ref, k_ref, v_ref, o_ref = refs
    tq = tk = ATTN_TILE
    qi = pl.program_id(2)
    scale = HEAD_DIM ** -0.5
    q = q_ref[0]
    qs = jnp.concatenate([q[:, h * HEAD_DIM:(h + 1) * HEAD_DIM] for h in range(NSA_HG)], axis=0)
    mm_dt, mm_prec = (F32, lax.Precision.HIGHEST) if mode == "cmp" else (BF16, None)
    qs = qs.astype(mm_dt)
    tpos = qi * tq + lax.broadcasted_iota(jnp.int32, (NSA_HG * tq, tk), 0) % tq
    kiota = lax.broadcasted_iota(jnp.int32, (NSA_HG * tq, tk), 1)

    def scores(kj):
        k_t = k_ref[0, pl.ds(pl.multiple_of(kj * tk, tk), tk), :].astype(mm_dt)
        s = lax.dot_general(qs, k_t, (((1,), (1,)), ((), ())), precision=mm_prec,
                            preferred_element_type=F32) * scale
        return s, kiota + kj * tk

    def finish(acc, l):
        o = acc / jnp.maximum(l, TINY)
        o_ref[0] = jnp.concatenate([o[h * tq:(h + 1) * tq] for h in range(NSA_HG)], axis=1)

    if mode == "cmp":
        s, kidx = scores(0)
        mask = (kidx * CMP_STRIDE + (CMP_LEN - 1) <= tpos) & (kidx < n_keys)
        s = jnp.where(mask, s, NEG)
        p = jnp.where(mask, jnp.exp(s - jnp.max(s, axis=-1, keepdims=True)), 0.0)
        p = p / jnp.maximum(jnp.sum(p, axis=-1, keepdims=True), TINY)
        p_ref[0, 0] = sum(p[h * tq:(h + 1) * tq] for h in range(NSA_HG))
        o = jnp.dot(p.astype(BF16), v_ref[0].astype(BF16), preferred_element_type=F32)
        o_ref[0] = jnp.concatenate([o[h * tq:(h + 1) * tq] for h in range(NSA_HG)], axis=1)
        return

    if mode == "sel":
        sel = sel_ref[0, 0].astype(BF16)
        n_blk = sel.shape[1]
        blk_row = lax.broadcasted_iota(jnp.int32, (n_blk, tk), 0)
        blk_col = lax.broadcasted_iota(jnp.int32, (n_blk, tk), 1)
        lo = 0
    else:
        lo = jnp.maximum(qi - WINDOW // tk, 0)

    def body(kj, carry):
        m, l, acc = carry
        s, kpos = scores(kj)
        mask = kpos <= tpos
        if mode == "sel":
            expand = jnp.where((blk_col + kj * tk) // SEL_BLOCK == blk_row, 1.0, 0.0).astype(BF16)
            picked = jnp.dot(sel, expand, preferred_element_type=F32) > 0.5
            mask = mask & jnp.concatenate([picked] * NSA_HG, axis=0)
        else:
            mask = mask & (kpos >= tpos - WINDOW)
        s = jnp.where(mask, s, NEG)
        m_new = jnp.maximum(m, jnp.max(s, axis=-1, keepdims=True))
        alpha = jnp.exp(m - m_new)
        p = jnp.where(mask, jnp.exp(s - m_new), 0.0)
        l = alpha * l + jnp.sum(p, axis=-1, keepdims=True)
        v_t = v_ref[0, pl.ds(pl.multiple_of(kj * tk, tk), tk), :].astype(BF16)
        acc = alpha * acc + jnp.dot(p.astype(BF16), v_t, preferred_element_type=F32)
        return m_new, l, acc

    rows = NSA_HG * tq
    init = (jnp.full((rows, 1), NEG, F32), jnp.zeros((rows, 1), F32), jnp.zeros((rows, HEAD_DIM), F32))
    _, l, acc = lax.fori_loop(lo, qi + 1, body, init)
    finish(acc, l)


def _attn(mode, q, k_arr, k_lane_block, v_arr, v_lane_block, sel=None, n_keys=None):
    bsz, n, _ = q.shape
    length = k_arr.shape[1]
    tq = ATTN_TILE
    assert n % tq == 0 and length % ATTN_TILE == 0
    gw = NSA_HG * HEAD_DIM
    q_spec = pl.BlockSpec((1, tq, gw), lambda bi, gi, qi: (bi, qi, gi))
    k_spec = pl.BlockSpec((1, length, HEAD_DIM), lambda bi, gi, qi: (bi, 0, k_lane_block(gi)))
    v_spec = pl.BlockSpec((1, length, HEAD_DIM), lambda bi, gi, qi: (bi, 0, v_lane_block(gi)))
    in_specs = [q_spec, k_spec, v_spec]
    args = [q, k_arr, v_arr]
    out_specs = [q_spec]
    out_shape = [jax.ShapeDtypeStruct((bsz, n, NSA_HEADS * HEAD_DIM), F32)]
    if mode == "sel":
        n_blk = sel.shape[-1]
        in_specs.append(pl.BlockSpec((1, 1, tq, n_blk), lambda bi, gi, qi: (bi, gi, qi, 0)))
        args.append(sel)
    if mode == "cmp":
        out_specs.append(pl.BlockSpec((1, 1, tq, ATTN_TILE), lambda bi, gi, qi: (bi, gi, qi, 0)))
        out_shape.append(jax.ShapeDtypeStruct((bsz, NSA_KV, n, ATTN_TILE), F32))
    out = pl.pallas_call(
        functools.partial(_attn_kernel, mode=mode, n_keys=n_keys),
        grid=(bsz, NSA_KV, n // tq),
        in_specs=in_specs,
        out_specs=out_specs,
        out_shape=out_shape,
        compiler_params=pltpu.CompilerParams(
            dimension_semantics=("arbitrary", "arbitrary", "arbitrary"),
            vmem_limit_bytes=VMEM_LIMIT_BYTES),
    )(*args)
    return out if mode == "cmp" else out[0]


def _rms_norm(x, g):
    xf = x.astype(F32)
    y = xf * lax.rsqrt(jnp.mean(xf * xf, axis=-1, keepdims=True) + NORM_EPS)
    return (y * g.astype(F32)).astype(x.dtype)


def _masked_softmax(s, mask):
    s = jnp.where(mask, s.astype(F32), NEG)
    p = jnp.where(mask, jnp.exp(s - jnp.max(s, axis=-1, keepdims=True)), 0.0)
    return p / jnp.maximum(jnp.sum(p, axis=-1, keepdims=True), TINY)


def _partial_rope(x, pos):
    half = ROT_DIM // 2
    inv_freq = jnp.power(ROPE_THETA, -jnp.arange(half, dtype=F32) / half)
    ang = pos.astype(F32)[:, None] * inv_freq[None, :]
    cos = jnp.cos(ang)[:, None, :]
    sin = jnp.sin(ang)[:, None, :]
    xr = x[..., :ROT_DIM].astype(F32)
    x1, x2 = xr[..., :half], xr[..., half:]
    rot = jnp.concatenate([x1 * cos - x2 * sin, x2 * cos + x1 * sin], axis=-1)
    return jnp.concatenate([rot.astype(x.dtype), x[..., ROT_DIM:]], axis=-1)


def _causal_dwconv(x, buf, w, b):
    width = w.shape[0]
    n = x.shape[1]
    xx = jnp.concatenate([buf.astype(x.dtype), x], axis=1)
    y = b + sum(xx[:, j:j + n] * w[j] for j in range(width))
    return y, xx[:, xx.shape[1] - (width - 1):]


def _rglru_mixer(x_in, gate_in, conv_buf, h0, lp):
    b, n, _ = x_in.shape
    xc, conv_new = _causal_dwconv(x_in, conv_buf, lp["lru_conv_w"], lp["lru_conv_b"])
    xh = xc.reshape(b, n, LRU_HEADS, LRU_HD)
    r = jax.nn.sigmoid((jnp.einsum("bnhi,hij->bnhj", xh, lp["lru_wa"]).reshape(b, n, BR_W)
                        + lp["lru_ba"]).astype(F32))
    i = jax.nn.sigmoid((jnp.einsum("bnhi,hij->bnhj", xh, lp["lru_wx"]).reshape(b, n, BR_W)
                        + lp["lru_bx"]).astype(F32))
    log_a = -LRU_C * r * jax.nn.softplus(-lp["lru_lambda"].astype(F32))
    a = jnp.exp(log_a)
    u = jnp.sqrt(-jnp.expm1(2.0 * log_a)) * (i * xc.astype(F32))
    h = _lru_scan(a, u, h0.astype(F32))
    y = (h * jax.nn.gelu(gate_in.astype(F32))).astype(x_in.dtype)
    return y, conv_new, h[:, -1].astype(x_in.dtype)


def _compress_blocks(rows, pe, w1, w2):
    b, length, g, d = rows.shape
    chunks = rows.reshape(b, length // CMP_STRIDE, CMP_STRIDE, g, d)
    w1r = w1.reshape(CMP_LEN, d, CMP_HIDDEN)
    first = jnp.einsum("bcpgd,pde->bcge", chunks, w1r[:CMP_STRIDE])
    second = jnp.einsum("bcpgd,pde->bcge", chunks, w1r[CMP_STRIDE:])
    pe_term = jnp.einsum("pd,pde->e", pe, w1r)
    hidden = jax.nn.gelu(first[:, :-1] + second[:, 1:] + pe_term)
    return jnp.einsum("bcge,ed->bcgd", hidden, w2)


def _nsa_mixer_prompt(q, nsa_rows, win_rows, gates, lp):
    b, n = q.shape[0], q.shape[1]
    tq = jnp.arange(n)
    q2 = q.reshape(b, n, NSA_HEADS * HEAD_DIM)
    rows2 = nsa_rows.reshape(b, n, 4 * NSA_KV * HEAD_DIM)
    win2 = win_rows.reshape(b, n, 2 * NSA_KV * HEAD_DIM)
    k_cmp = _compress_blocks(nsa_rows[:, :, 0], lp["cmp_pe"][0], lp["cmp_w1"][0], lp["cmp_w2"][0])
    v_cmp = _compress_blocks(nsa_rows[:, :, 1], lp["cmp_pe"][1], lp["cmp_w1"][1], lp["cmp_w2"][1])
    n_cmp = k_cmp.shape[1]
    assert n_cmp <= ATTN_TILE
    cmp_end = jnp.arange(n_cmp) * CMP_STRIDE + (CMP_LEN - 1)
    k_cmp = _partial_rope(_rms_norm(k_cmp, lp["qk_norm"][1]), cmp_end)
    pad = ((0, 0), (0, ATTN_TILE - n_cmp), (0, 0))
    k_cmp2 = jnp.pad(k_cmp.reshape(b, n_cmp, NSA_KV * HEAD_DIM), pad)
    v_cmp2 = jnp.pad(v_cmp.reshape(b, n_cmp, NSA_KV * HEAD_DIM), pad)
    o_c, psum = _attn("cmp", q2, k_cmp2, lambda g: g, v_cmp2, lambda g: g, n_keys=n_cmp)
    n_slc = n // SEL_BLOCK
    ratio = SEL_BLOCK // CMP_STRIDE
    imp = jnp.pad(psum[..., :n_cmp], ((0, 0), (0, 0), (0, 0), (1, 1)))
    imp_slc = (imp[..., :ratio * n_slc].reshape(b, NSA_KV, n, n_slc, ratio).sum(-1)
               + imp[..., ratio::ratio])
    blk = jnp.arange(n_slc)[None, :]
    cur = (tq // SEL_BLOCK)[:, None]
    valid = blk * SEL_BLOCK <= tq[:, None]
    forced = (blk == 0) | (blk == cur) | (blk == cur - 1)
    score = jnp.where(valid, imp_slc + jnp.where(forced, FORCE_BONUS, 0.0), NEG)
    k_sel = min(N_SEL, n_slc)
    top_val, top_idx = lax.top_k(score, k_sel)
    top_ok = top_val > 0.5 * NEG
    sel = jnp.any((top_idx[..., None] == jnp.arange(n_slc)) & top_ok[..., None], axis=-2).astype(F32)
    o_s = _attn("sel", q2, rows2, lambda g: 2 * NSA_KV + g, rows2, lambda g: 3 * NSA_KV + g, sel=sel)
    o_w = _attn("win", q2, win2, lambda g: g, win2, lambda g: NSA_KV + g)
    g4 = gates.astype(q.dtype)
    o4 = lambda o: o.reshape(b, n, NSA_HEADS, HEAD_DIM)
    o = g4[..., 0:1] * o4(o_c) + g4[..., 1:2] * o4(o_s) + g4[..., 2:3] * o4(o_w)
    return o.reshape(b, n, NSA_HEADS * HEAD_DIM)


def _nsa_mixer_paged(q, nsa_rows, win_all, gates, cache, layer, page_table, lp):
    b, n = q.shape[0], q.shape[1]
    past_len = page_table.shape[1] * PAGE_SIZE
    assert n < CMP_STRIDE and n <= SEL_BLOCK and win_all.shape[1] == WINDOW + n
    scale = HEAD_DIM ** -0.5
    tq = past_len + jnp.arange(n)
    q5 = q.reshape(b, n, NSA_KV, NSA_HG, HEAD_DIM)
    first, second = _compress_pages(cache, layer, page_table, lp["cmp_w1"])
    w1r = lp["cmp_w1"].reshape(2, CMP_LEN, HEAD_DIM, CMP_HIDDEN)
    pe_term = jnp.einsum("kpd,kpde->ke", lp["cmp_pe"], w1r)
    hidden = jax.nn.gelu(first[:, :, :, :-1] + second[:, :, :, 1:] + pe_term[None, :, None, None, :])
    kv_cmp = jnp.einsum("bkgce,ked->bkgcd", hidden, lp["cmp_w2"])
    n_cmp = kv_cmp.shape[3]
    cmp_end = jnp.arange(n_cmp) * CMP_STRIDE + (CMP_LEN - 1)
    k_cmp = _partial_rope(_rms_norm(jnp.moveaxis(kv_cmp[:, 0], 1, 2), lp["qk_norm"][1]), cmp_end)
    v_cmp = jnp.moveaxis(kv_cmp[:, 1], 1, 2)
    s_c = jnp.einsum("bqghd,bkgd->bghqk", q5, k_cmp) * scale
    p_c = _masked_softmax(s_c, cmp_end[None, :] <= tq[:, None])
    o_c = jnp.einsum("bghqk,bkgd->bqghd", p_c.astype(q.dtype), v_cmp)
    n_slc = -(-(past_len + n) // SEL_BLOCK)
    ratio = SEL_BLOCK // CMP_STRIDE
    imp = jnp.pad(p_c.sum(axis=2), ((0, 0), (0, 0), (0, 0), (1, ratio * n_slc - n_cmp)))
    imp_slc = (imp[..., :ratio * n_slc].reshape(b, NSA_KV, n, n_slc, ratio).sum(-1)
               + imp[..., ratio::ratio])
    blk = jnp.arange(n_slc)[None, :]
    cur = (tq // SEL_BLOCK)[:, None]
    valid = blk * SEL_BLOCK <= tq[:, None]
    forced = (blk == 0) | (blk == cur) | (blk == cur - 1)
    score = jnp.where(valid, imp_slc + jnp.where(forced, FORCE_BONUS, 0.0), NEG)
    k_sel = min(N_SEL, n_slc)
    top_val, top_idx = lax.top_k(score, k_sel)
    top_ok = top_val > 0.5 * NEG
    n_past_blk = past_len // SEL_BLOCK
    per_page = PAGE_SIZE // SEL_BLOCK
    idx_c = jnp.minimum(top_idx, n_past_blk - 1)
    bi = jnp.arange(b)[:, None, None, None]
    gi = jnp.arange(NSA_KV)[None, :, None, None]
    page = page_table[bi, idx_c // per_page]
    row0 = (idx_c % per_page) * SEL_BLOCK
    cache_l = cache[layer]

    def take(kind):
        one = lambda pg, r0, g: lax.dynamic_slice(
            cache_l, (pg, r0, kind, g, 0), (1, SEL_BLOCK, 1, 1, HEAD_DIM)).reshape(SEL_BLOCK, HEAD_DIM)
        flat = jax.vmap(one)(page.reshape(-1), row0.reshape(-1), jnp.broadcast_to(gi, page.shape).reshape(-1))
        past_blk = flat.reshape(b, NSA_KV, n, k_sel, SEL_BLOCK, HEAD_DIM)
        tail = jnp.pad(jnp.moveaxis(nsa_rows[:, :, kind], 1, 2), ((0, 0), (0, 0), (0, SEL_BLOCK - n), (0, 0)))
        return jnp.where((top_idx >= n_past_blk)[..., None, None], tail[:, :, None, None], past_blk)

    m = k_sel * SEL_BLOCK
    kg = take(2).reshape(b, NSA_KV, n, m, HEAD_DIM)
    vg = take(3).reshape(b, NSA_KV, n, m, HEAD_DIM)
    s = jnp.einsum("bqghd,bgqmd->bghqm", q5, kg) * scale
    kpos = (top_idx[..., None] * SEL_BLOCK + jnp.arange(SEL_BLOCK)).reshape(b, NSA_KV, n, m)
    mask = (kpos <= tq[None, None, :, None]) & jnp.repeat(top_ok, SEL_BLOCK, axis=-1)
    o_s = jnp.einsum("bghqm,bgqmd->bqghd", _masked_softmax(s, mask[:, :, None]).astype(q.dtype), vg)
    kpos_w = past_len - WINDOW + jnp.arange(WINDOW + n)
    mask_w = ((kpos_w[None, :] <= tq[:, None]) & (kpos_w[None, :] >= tq[:, None] - WINDOW)
              & (kpos_w[None, :] >= 0))
    s_w = jnp.einsum("bqghd,bkgd->bghqk", q5, win_all[:, :, 0]) * scale
    o_w = jnp.einsum("bghqk,bkgd->bqghd", _masked_softmax(s_w, mask_w).astype(q.dtype), win_all[:, :, 1])
    g5 = gates.reshape(b, n, NSA_KV, NSA_HG, 3).astype(q.dtype)
    o = g5[..., 0:1] * o_c + g5[..., 1:2] * o_s + g5[..., 2:3] * o_w
    return o.reshape(b, n, NSA_HEADS * HEAD_DIM)


def _rwkv7_mixer(z, shift_buf, s0, lp):
    b, n, _ = z.shape
    prev = jnp.concatenate([shift_buf[:, None].astype(z.dtype), z[:, :-1]], axis=1)
    zm = z + (prev - z) * lp["rwkv_mu"]
    r, k, v, wd, ad, gd = jnp.split(zm, RWKV_OFFSETS, axis=-1)
    w_log = -jax.nn.softplus(-(lp["rwkv_w0"] + jnp.tanh(wd) @ lp["rwkv_w2"]).astype(F32)) - 0.5
    decay = jnp.exp(-jnp.exp(w_log))
    a = jax.nn.sigmoid((lp["rwkv_a0"] + ad @ lp["rwkv_a2"]).astype(F32))
    g = jax.nn.sigmoid(gd) @ lp["rwkv_g2"]
    hs = lambda t: t.astype(F32).reshape(b, n, RWKV_HEADS, RWKV_HD)
    kk = hs(k * lp["rwkv_kk"])
    kk = kk / jnp.maximum(jnp.sqrt(jnp.sum(kk * kk, axis=-1, keepdims=True)), 1e-12)
    k_h = hs(k.astype(F32) * (1.0 + (a - 1.0) * lp["rwkv_ka"].astype(F32)))
    r_h, v_h, a_h = hs(r), hs(v), hs(a)
    flat = lambda t: t.reshape(b, n, BR_W)
    y, s_fin = _rwkv_scan(r.astype(F32), decay, flat(k_h), v.astype(F32), flat(kk), flat(-(kk * a_h)),
                          s0.astype(F32))
    y = y.reshape(b, n, RWKV_HEADS, RWKV_HD)
    mu = jnp.mean(y, axis=-1, keepdims=True)
    var = jnp.mean((y - mu) ** 2, axis=-1, keepdims=True)
    y = ((y - mu) * lax.rsqrt(var + GN_EPS)).reshape(b, n, BR_W) * lp["rwkv_gn_g"] + lp["rwkv_gn_b"]
    bonus = jnp.sum(r_h * k_h * lp["rwkv_rk"].astype(F32), axis=-1, keepdims=True) * v_h
    y = y + bonus.reshape(b, n, BR_W)
    out = (y * g.astype(F32)).astype(z.dtype)
    return out, s_fin.astype(z.dtype), z[:, -1]


def _trunk_layer(x, lp, st, q_off, win_keep):
    b, n, _ = x.shape
    dt = x.dtype
    t = b * n
    xn = _rms_norm(x, lp["norm1"]).reshape(t, D_MODEL).astype(BF16)
    (lru_x, lru_gate, q, kv_cmp, kv_sel, kv_win, nsa_gate, rwkv_in,
     merge_gate) = [_proj(xn, w).reshape(b, n, w.shape[1]) for w in lp["w_in"]]
    y_a, lru_conv_new, lru_h_new = _rglru_mixer(lru_x, lru_gate, st["lru_conv"], st["lru_h"], lp)
    pos = q_off + jnp.arange(n)
    qn = lp["qk_norm"]
    q = _partial_rope(_rms_norm(q.reshape(b, n, NSA_HEADS, HEAD_DIM), qn[0]), pos)
    kv_cmp = kv_cmp.reshape(b, n, 2, NSA_KV, HEAD_DIM)
    kv_sel = kv_sel.reshape(b, n, 2, NSA_KV, HEAD_DIM)
    kv_win = kv_win.reshape(b, n, 2, NSA_KV, HEAD_DIM)
    k_sel = _partial_rope(_rms_norm(kv_sel[:, :, 0], qn[2]), pos)
    k_win = _partial_rope(_rms_norm(kv_win[:, :, 0], qn[3]), pos)
    nsa_rows = jnp.stack([kv_cmp[:, :, 0], kv_cmp[:, :, 1], k_sel, kv_sel[:, :, 1]], axis=2)
    win_rows = jnp.stack([k_win, kv_win[:, :, 1]], axis=2)
    win_all = jnp.concatenate([st["win"].astype(dt), win_rows], axis=1)
    win_new = win_all[:, win_all.shape[1] - win_keep:]
    nsa_gates = jax.nn.sigmoid(nsa_gate.reshape(b, n, NSA_HEADS, 3))
    if st["paged_past"] is None:
        assert q_off == 0 and st["win"].shape[1] == 0
        y_b = _nsa_mixer_prompt(q, nsa_rows, win_rows, nsa_gates, lp)
    else:
        cache, layer, page_table = st["paged_past"]
        assert q_off == page_table.shape[1] * PAGE_SIZE
        y_b = _nsa_mixer_paged(q, nsa_rows, win_all, nsa_gates, cache, layer, page_table, lp)
    y_c, rwkv_s_new, rwkv_shift_new = _rwkv7_mixer(rwkv_in, st["rwkv_shift"], st["rwkv_s"], lp)
    gate = jax.nn.sigmoid(merge_gate.reshape(b, n, 3, D_MODEL))
    mix = 0.0
    for bi, y in enumerate((y_a, y_b, y_c)):
        pb = _proj(y.reshape(t, BR_W), lp["w_branch"][bi]).reshape(b, n, D_MODEL)
        mix = mix + gate[:, :, bi] * pb
    x = x + _proj(mix.reshape(t, D_MODEL), lp["w_out"]).reshape(b, n, D_MODEL)
    xn2 = _rms_norm(x, lp["norm2"]).reshape(t, D_MODEL).astype(BF16)
    hg, hv = [_proj(xn2, w).reshape(b, n, D_FF) for w in lp["ffn_w_in"]]
    hc, ffn_conv_new = _causal_dwconv(hg, st["ffn_conv"], lp["ffn_conv_w"], lp["ffn_conv_b"])
    act = jax.nn.gelu(hc) * hv
    x = x + _proj(act.reshape(t, D_FF), lp["ffn_w_down"]).reshape(b, n, D_MODEL)
    new_state = {"nsa": nsa_rows, "win": win_new, "lru_h": lru_h_new.astype(dt),
                 "lru_conv": lru_conv_new.astype(dt), "rwkv_s": rwkv_s_new.astype(dt),
                 "rwkv_shift": rwkv_shift_new.astype(dt), "ffn_conv": ffn_conv_new.astype(dt)}
    return x, new_state


def kernel(x_prompt, x_sample, cache_nsa_kv, state_win_kv, state_lru_h, state_lru_conv,
           state_rwkv_s, state_rwkv_shift, state_ffn_conv, page_table,
           norm1_g, norm2_g, w_in, lru_conv_w, lru_conv_b, lru_wa, lru_ba, lru_wx, lru_bx,
           lru_lambda, qk_norm_g, cmp_pe, cmp_w1, cmp_w2, rwkv_mu, rwkv_w0, rwkv_w2,
           rwkv_a0, rwkv_a2, rwkv_g2, rwkv_kk, rwkv_ka, rwkv_rk, rwkv_gn_g, rwkv_gn_b,
           w_branch, w_out, ffn_w_in, ffn_conv_w, ffn_conv_b, ffn_w_down):
    dt = x_prompt.dtype
    bp, n_p, _ = x_prompt.shape
    bs = x_sample.shape[0]
    depth = w_in.shape[0]
    past_len = page_table.shape[1] * PAGE_SIZE
    xp, xs = x_prompt, x_sample
    new_p = {name: [] for name in STATE_NAMES}
    new_s = {name: [] for name in STATE_NAMES}
    for l in range(depth):
        in_bounds = (0,) + IN_OFFSETS + (D_IN,)
        w_in_segs = [w_in[l][:, a:z].astype(BF16) for a, z in zip(in_bounds[:-1], in_bounds[1:])]
        lp = {"norm1": norm1_g[l], "norm2": norm2_g[l], "w_in": w_in_segs,
              "lru_conv_w": lru_conv_w[l], "lru_conv_b": lru_conv_b[l], "lru_wa": lru_wa[l],
              "lru_ba": lru_ba[l], "lru_wx": lru_wx[l], "lru_bx": lru_bx[l],
              "lru_lambda": lru_lambda[l], "qk_norm": qk_norm_g[l], "cmp_pe": cmp_pe[l],
              "cmp_w1": cmp_w1[l], "cmp_w2": cmp_w2[l], "rwkv_mu": rwkv_mu[l],
              "rwkv_w0": rwkv_w0[l], "rwkv_w2": rwkv_w2[l], "rwkv_a0": rwkv_a0[l],
              "rwkv_a2": rwkv_a2[l], "rwkv_g2": rwkv_g2[l], "rwkv_kk": rwkv_kk[l],
              "rwkv_ka": rwkv_ka[l], "rwkv_rk": rwkv_rk[l], "rwkv_gn_g": rwkv_gn_g[l],
              "rwkv_gn_b": rwkv_gn_b[l], "w_branch": w_branch[l].astype(BF16),
              "w_out": w_out[l].astype(BF16), "ffn_w_in": [ffn_w_in[l][:, :D_FF].astype(BF16), ffn_w_in[l][:, D_FF:].astype(BF16)],
              "ffn_conv_w": ffn_conv_w[l], "ffn_conv_b": ffn_conv_b[l],
              "ffn_w_down": ffn_w_down[l].astype(BF16)}
        st_p = {"paged_past": None,
                "win": jnp.zeros((bp, 0, 2, NSA_KV, HEAD_DIM), dt),
                "lru_h": jnp.zeros((bp, BR_W), dt),
                "lru_conv": jnp.zeros((bp, LRU_CONV - 1, BR_W), dt),
                "rwkv_s": jnp.zeros((bp, RWKV_HEADS, RWKV_HD, RWKV_HD), dt),
                "rwkv_shift": jnp.zeros((bp, RWKV_IN), dt),
                "ffn_conv": jnp.zeros((bp, FFN_CONV - 1, D_FF), dt)}
        xp, out_p = _trunk_layer(xp, lp, st_p, 0, min(WINDOW, n_p))
        st_s = {"paged_past": (cache_nsa_kv, l, page_table), "win": state_win_kv[l], "lru_h": state_lru_h[l],
                "lru_conv": state_lru_conv[l], "rwkv_s": state_rwkv_s[l],
                "rwkv_shift": state_rwkv_shift[l], "ffn_conv": state_ffn_conv[l]}
        xs, out_s = _trunk_layer(xs, lp, st_s, past_len, state_win_kv.shape[2])
        for name in STATE_NAMES:
            new_p[name].append(out_p[name])
            new_s[name].append(out_s[name])
    sp = {name: jnp.stack(new_p[name], axis=0) for name in STATE_NAMES}
    ss = {name: jnp.stack(new_s[name], axis=0) for name in STATE_NAMES}
    return (xp, xs, sp["nsa"], sp["win"], sp["lru_h"], sp["lru_conv"], sp["rwkv_s"],
            sp["rwkv_shift"], sp["ffn_conv"], ss["nsa"], ss["win"], ss["lru_h"], ss["lru_conv"],
            ss["rwkv_s"], ss["rwkv_shift"], ss["ffn_conv"])
```

```python
import functools
import math

import jax
import jax.numpy as jnp
import numpy as np
from jax import lax
from jax.experimental import pallas as pl
from jax.experimental.pallas import tpu as pltpu

F32 = jnp.float32
BF16 = jnp.bfloat16

D_MODEL = 4096
PAGE_SIZE = 128
BR_W = D_MODEL // 2
LRU_HEADS = 16
LRU_HD = BR_W // LRU_HEADS
LRU_CONV = 4
LRU_C = 8.0
HEAD_DIM = 128
NSA_HEADS = BR_W // HEAD_DIM
NSA_KV = 4
NSA_HG = NSA_HEADS // NSA_KV
ROT_DIM = HEAD_DIM // 4
ROPE_THETA = 500000.0
CMP_LEN = 32
CMP_STRIDE = 16
CMP_HIDDEN = 2 * HEAD_DIM
SEL_BLOCK = 64
N_SEL = 16
WINDOW = 512
FORCE_BONUS = 1e4
NEG = -1e30
TINY = 1e-30
RWKV_HD = 64
RWKV_HEADS = BR_W // RWKV_HD
LORA_W = 64
LORA_A = 64
LORA_G = 128
RWKV_SIZES = (BR_W, BR_W, BR_W, LORA_W, LORA_A, LORA_G)
RWKV_IN = sum(RWKV_SIZES)
RWKV_OFFSETS = tuple(int(v) for v in np.cumsum(RWKV_SIZES)[:-1])
GN_EPS = 64e-5
D_FF = ((8 * D_MODEL // 3 + 255) // 256) * 256
FFN_CONV = 3
NORM_EPS = 1e-6
IN_SIZES = (BR_W, BR_W, NSA_HEADS * HEAD_DIM, 2 * NSA_KV * HEAD_DIM, 2 * NSA_KV * HEAD_DIM,
            2 * NSA_KV * HEAD_DIM, 3 * NSA_HEADS, RWKV_IN, 3 * D_MODEL)
D_IN = sum(IN_SIZES)
IN_OFFSETS = tuple(int(v) for v in np.cumsum(IN_SIZES)[:-1])
STATE_NAMES = ("nsa", "win", "lru_h", "lru_conv", "rwkv_s", "rwkv_shift", "ffn_conv")

VMEM_LIMIT_BYTES = 56 * 1024 * 1024


def _mm_kernel(x_ref, w_ref, o_ref, *, nk):
    part = jnp.dot(x_ref[...], w_ref[...], preferred_element_type=F32)
    if nk == 1:
        o_ref[...] = part
    else:
        k = pl.program_id(2)

        @pl.when(k == 0)
        def _():
            o_ref[...] = part

        @pl.when(k != 0)
        def _():
            o_ref[...] += part


def _matmul(x, w, *, tm, tn, tk=None):
    m, kdim = x.shape
    n = w.shape[1]
    tk = kdim if tk is None else tk
    assert m % tm == 0 and kdim % tk == 0
    nk = kdim // tk
    grid = (m // tm, pl.cdiv(n, tn), nk)
    return pl.pallas_call(
        functools.partial(_mm_kernel, nk=nk),
        grid=grid,
        in_specs=[pl.BlockSpec((tm, tk), lambda i, j, k: (i, k)),
                  pl.BlockSpec((tk, tn), lambda i, j, k: (k, j))],
        out_specs=pl.BlockSpec((tm, tn), lambda i, j, k: (i, j)),
        out_shape=jax.ShapeDtypeStruct((m, n), F32),
        compiler_params=pltpu.CompilerParams(
            dimension_semantics=("arbitrary", "arbitrary", "arbitrary"),
            vmem_limit_bytes=VMEM_LIMIT_BYTES),
    )(x, w)


def _proj(x2d, w_bf16):
    m, kdim = x2d.shape
    n = w_bf16.shape[1]
    xb = x2d.astype(BF16)
    tm, tn = (1024, 512) if m >= 1024 else (m, 1024)
    tn = min(tn, n)
    tk = kdim if kdim <= 4096 else kdim // 2
    return _matmul(xb, w_bf16, tm=tm, tn=tn, tk=tk)


RWKV_HEAD_BLOCK = 16
LANES = 128


def _rwkv_scan_kernel(r_ref, w_ref, k_ref, v_ref, kk_ref, b_ref, s0_ref, y_ref, s_ref, *, tt, hb):
    @pl.when(pl.program_id(2) == 0)
    def _():
        s_ref[...] = s0_ref[...]

    hd = RWKV_HD
    shift = hd.bit_length() - 1
    iota = lambda shape, d: lax.broadcasted_iota(jnp.int32, shape, d)
    eye2 = ((iota((hd, LANES), 1) & (hd - 1)) == iota((hd, LANES), 0)).astype(F32)
    blk_ones = (lax.shift_right_logical(iota((LANES, LANES), 0), shift)
                == lax.shift_right_logical(iota((LANES, LANES), 1), shift)).astype(BF16)
    sub = min(tt, 8)
    pairs = range(hb // 2)
    ps = [slice(p * hd, (p + 1) * hd) for p in pairs]

    def head_sums(x):
        return jnp.dot(x.astype(BF16), blk_ones, preferred_element_type=F32)

    def head_sums_f32(x):
        hi = x.astype(BF16)
        return jnp.dot(hi, blk_ones, preferred_element_type=F32) + head_sums(x - hi.astype(F32))

    def block(tb, carry):
        rows = pl.ds(pl.multiple_of(tb * sub, sub), sub)
        blk = [ref[0, rows, :] for ref in (kk_ref, w_ref, b_ref, k_ref, v_ref, r_ref)]
        vec = lambda q, i, p: blk[q][i:i + 1, p * LANES:(p + 1) * LANES]
        ys = [[] for _ in pairs]
        for i in range(sub):
            s = [s_ref[0, ps[p], :] for p in pairs]
            sa = head_sums_f32(jnp.concatenate([s[p] * vec(0, i, p) for p in pairs], axis=0))
            vcol = head_sums(jnp.concatenate([eye2 * vec(4, i, p) for p in pairs], axis=0))
            for p in pairs:
                s[p] = s[p] * vec(1, i, p) + sa[ps[p]] * vec(2, i, p) + vcol[ps[p]] * vec(3, i, p)
                s_ref[0, ps[p], :] = s[p]
            ycol = head_sums(jnp.concatenate([s[p] * vec(5, i, p) for p in pairs], axis=0))
            for p in pairs:
                ys[p].append(jnp.sum(ycol[ps[p]] * eye2, axis=0, keepdims=True))
        for p in pairs:
            y_ref[0, rows, p * LANES:(p + 1) * LANES] = jnp.concatenate(ys[p], axis=0)
        return carry

    lax.fori_loop(0, tt // sub, block, 0)


def _rwkv_scan(r, w, k, v, kk, b, s0):
    bsz, n, _ = r.shape
    hb, hd = RWKV_HEAD_BLOCK, RWKV_HD
    assert 2 * hd == LANES and RWKV_HEADS % hb == 0 and hb % 2 == 0
    tt = math.gcd(n, 256)
    n_pairs = RWKV_HEADS // 2
    vec = pl.BlockSpec((1, tt, hb * hd), lambda bi, hi, ti: (bi, ti, hi))
    st = pl.BlockSpec((1, hb // 2 * hd, LANES), lambda bi, hi, ti: (bi, hi, 0))
    s0p = s0.reshape(bsz, n_pairs, 2, hd, hd).transpose(0, 1, 3, 2, 4).reshape(bsz, n_pairs * hd, LANES)
    y, s_fin = pl.pallas_call(
        functools.partial(_rwkv_scan_kernel, tt=tt, hb=hb),
        grid=(bsz, RWKV_HEADS // hb, n // tt),
        in_specs=[vec] * 6 + [st],
        out_specs=[vec, st],
        out_shape=[jax.ShapeDtypeStruct((bsz, n, BR_W), F32),
                   jax.ShapeDtypeStruct((bsz, n_pairs * hd, LANES), F32)],
        compiler_params=pltpu.CompilerParams(
            dimension_semantics=("arbitrary", "arbitrary", "arbitrary"),
            vmem_limit_bytes=VMEM_LIMIT_BYTES),
    )(r, w, k, v, kk, b, s0p)
    s_fin = s_fin.reshape(bsz, n_pairs, hd, 2, hd).transpose(0, 1, 3, 2, 4)
    return y, s_fin.reshape(bsz, RWKV_HEADS, hd, hd)


def _lru_scan_kernel(a_ref, u_ref, h0_ref, h_ref, carry_ref, *, tt):
    @pl.when(pl.program_id(1) == 0)
    def _():
        carry_ref[...] = h0_ref[0]

    sub = min(tt, 8)

    def block(tb, h):
        rows = pl.ds(pl.multiple_of(tb * sub, sub), sub)
        a8, u8 = a_ref[0, rows, :], u_ref[0, rows, :]
        outs = []
        for i in range(sub):
            h = a8[i:i + 1] * h + u8[i:i + 1]
            outs.append(h)
        h_ref[0, rows, :] = jnp.concatenate(outs, axis=0)
        return h

    carry_ref[...] = lax.fori_loop(0, tt // sub, block, carry_ref[...])


def _lru_scan(a, u, h0):
    bsz, n, c = a.shape
    tt = math.gcd(n, 256)
    vec = pl.BlockSpec((1, tt, c), lambda bi, ti: (bi, ti, 0))
    return pl.pallas_call(
        functools.partial(_lru_scan_kernel, tt=tt),
        grid=(bsz, n // tt),
        in_specs=[vec, vec, pl.BlockSpec((1, 1, c), lambda bi, ti: (bi, 0, 0))],
        out_specs=vec,
        out_shape=jax.ShapeDtypeStruct((bsz, n, c), F32),
        scratch_shapes=[pltpu.VMEM((1, c), F32)],
        compiler_params=pltpu.CompilerParams(
            dimension_semantics=("arbitrary", "arbitrary"),
            vmem_limit_bytes=VMEM_LIMIT_BYTES),
    )(a, u, h0.reshape(bsz, 1, c))


CMP_PAGES_PER_STEP = 8
ROW_COMBOS = 4 * NSA_KV


def _compress_pages_kernel(pt_ref, *refs, pp):
    del pt_ref
    pages, (w1_ref, first_ref, second_ref) = refs[:pp], refs[pp:]
    cpp = PAGE_SIZE // CMP_STRIDE
    for kind in range(2):
        acc_f = jnp.zeros((NSA_KV * pp * cpp, CMP_HIDDEN), F32)
        acc_s = jnp.zeros((NSA_KV * pp * cpp, CMP_HIDDEN), F32)
        for p in range(CMP_STRIDE):
            x = jnp.concatenate(
                [page[pl.ds(p * ROW_COMBOS + kind * NSA_KV + g, cpp, stride=CMP_STRIDE * ROW_COMBOS), :]
                 for g in range(NSA_KV) for page in pages], axis=0).astype(BF16)
            acc_f += jnp.dot(x, w1_ref[kind, p], preferred_element_type=F32)
            acc_s += jnp.dot(x, w1_ref[kind, CMP_STRIDE + p], preferred_element_type=F32)
        for g in range(NSA_KV):
            first_ref[0, kind, g] = acc_f[g * pp * cpp:(g + 1) * pp * cpp]
            second_ref[0, kind, g] = acc_s[g * pp * cpp:(g + 1) * pp * cpp]


def _compress_pages(cache, layer, page_table, w1):
    depth, n_pool = cache.shape[0], cache.shape[1]
    bsz, n_pages = page_table.shape
    pp = math.gcd(n_pages, CMP_PAGES_PER_STEP)
    cpp = PAGE_SIZE // CMP_STRIDE
    cache2 = cache.reshape(depth, n_pool, PAGE_SIZE * ROW_COMBOS, HEAD_DIM)
    w1r = w1.reshape(2, CMP_LEN, HEAD_DIM, CMP_HIDDEN).astype(BF16)

    def page_spec(k):
        return pl.BlockSpec((None, None, PAGE_SIZE * ROW_COMBOS, HEAD_DIM),
                            lambda bi, pi, pt: (layer, pt[bi, pi * pp + k], 0, 0))

    out_spec = pl.BlockSpec((1, 2, NSA_KV, pp * cpp, CMP_HIDDEN), lambda bi, pi, pt: (bi, 0, 0, pi, 0))
    out_sds = jax.ShapeDtypeStruct((bsz, 2, NSA_KV, n_pages * cpp, CMP_HIDDEN), F32)
    return pl.pallas_call(
        functools.partial(_compress_pages_kernel, pp=pp),
        grid_spec=pltpu.PrefetchScalarGridSpec(
            num_scalar_prefetch=1,
            grid=(bsz, n_pages // pp),
            in_specs=[page_spec(k) for k in range(pp)]
            + [pl.BlockSpec((2, CMP_LEN, HEAD_DIM, CMP_HIDDEN), lambda bi, pi, pt: (0, 0, 0, 0))],
            out_specs=[out_spec, out_spec]),
        out_shape=[out_sds, out_sds],
        compiler_params=pltpu.CompilerParams(
            dimension_semantics=("arbitrary", "arbitrary"),
            vmem_limit_bytes=VMEM_LIMIT_BYTES),
    )(page_table, *([cache2] * pp), w1r)


ATTN_TILE = 128


def _attn_kernel(*refs, mode, n_keys):
    if mode == "cmp":
        q_ref, k_ref, v_ref, o_ref, p_ref = refs
    elif mode == "sel":
        q_ref, k_ref, v_ref, sel_ref, o_ref = refs
    else:
        q_ref, k_ref, v_ref, o_ref = refs
    tq = tk = ATTN_TILE
    qi = pl.program_id(2)
    scale = HEAD_DIM ** -0.5
    q = q_ref[0]
    qs = jnp.concatenate([q[:, h * HEAD_DIM:(h + 1) * HEAD_DIM] for h in range(NSA_HG)], axis=0)
    mm_dt, mm_prec = (F32, lax.Precision.HIGHEST) if mode == "cmp" else (BF16, None)
    qs = qs.astype(mm_dt)
    tpos = qi * tq + lax.broadcasted_iota(jnp.int32, (NSA_HG * tq, tk), 0) % tq
    kiota = lax.broadcasted_iota(jnp.int32, (NSA_HG * tq, tk), 1)

    def scores(kj):
        k_t = k_ref[0, pl.ds(pl.multiple_of(kj * tk, tk), tk), :].astype(mm_dt)
        s = lax.dot_general(qs, k_t, (((1,), (1,)), ((), ())), precision=mm_prec,
                            preferred_element_type=F32) * scale
        return s, kiota + kj * tk

    def finish(acc, l):
        o = acc / jnp.maximum(l, TINY)
        o_ref[0] = jnp.concatenate([o[h * tq:(h + 1) * tq] for h in range(NSA_HG)], axis=1)

    if mode == "cmp":
        s, kidx = scores(0)
        mask = (kidx * CMP_STRIDE + (CMP_LEN - 1) <= tpos) & (kidx < n_keys)
        s = jnp.where(mask, s, NEG)
        p = jnp.where(mask, jnp.exp(s - jnp.max(s, axis=-1, keepdims=True)), 0.0)
        p = p / jnp.maximum(jnp.sum(p, axis=-1, keepdims=True), TINY)
        p_ref[0, 0] = sum(p[h * tq:(h + 1) * tq] for h in range(NSA_HG))
        o = jnp.dot(p.astype(BF16), v_ref[0].astype(BF16), preferred_element_type=F32)
        o_ref[0] = jnp.concatenate([o[h * tq:(h + 1) * tq] for h in range(NSA_HG)], axis=1)
        return

    if mode == "sel":
        sel = sel_ref[0, 0].astype(BF16)
        n_blk = sel.shape[1]
        blk_row = lax.broadcasted_iota(jnp.int32, (n_blk, tk), 0)
        blk_col = lax.broadcasted_iota(jnp.int32, (n_blk, tk), 1)
        lo = 0
    else:
        lo = jnp.maximum(qi - WINDOW // tk, 0)

    def body(kj, carry):
        m, l, acc = carry
        s, kpos = scores(kj)
        mask = kpos <= tpos
        if mode == "sel":
            expand = jnp.where((blk_col + kj * tk) // SEL_BLOCK == blk_row, 1.0, 0.0).astype(BF16)
            picked = jnp.dot(sel, expand, preferred_element_type=F32) > 0.5
            mask = mask & jnp.concatenate([picked] * NSA_HG, axis=0)
        else:
            mask = mask & (kpos >= tpos - WINDOW)
        s = jnp.where(mask, s, NEG)
        m_new = jnp.maximum(m, jnp.max(s, axis=-1, keepdims=True))
        alpha = jnp.exp(m - m_new)
        p = jnp.where(mask, jnp.exp(s - m_new), 0.0)
        l = alpha * l + jnp.sum(p, axis=-1, keepdims=True)
        v_t = v_ref[0, pl.ds(pl.multiple_of(kj * tk, tk), tk), :].astype(BF16)
        acc = alpha * acc + jnp.dot(p.astype(BF16), v_t, preferred_element_type=F32)
        return m_new, l, acc

    rows = NSA_HG * tq
    init = (jnp.full((rows, 1), NEG, F32), jnp.zeros((rows, 1), F32), jnp.zeros((rows, HEAD_DIM), F32))
    _, l, acc = lax.fori_loop(lo, qi + 1, body, init)
    finish(acc, l)


def _attn(mode, q, k_arr, k_lane_block, v_arr, v_lane_block, sel=None, n_keys=None):
    bsz, n, _ = q.shape
    length = k_arr.shape[1]
    tq = ATTN_TILE
    assert n % tq == 0 and length % ATTN_TILE == 0
    gw = NSA_HG * HEAD_DIM
    q_spec = pl.BlockSpec((1, tq, gw), lambda bi, gi, qi: (bi, qi, gi))
    k_spec = pl.BlockSpec((1, length, HEAD_DIM), lambda bi, gi, qi: (bi, 0, k_lane_block(gi)))
    v_spec = pl.BlockSpec((1, length, HEAD_DIM), lambda bi, gi, qi: (bi, 0, v_lane_block(gi)))
    in_specs = [q_spec, k_spec, v_spec]
    args = [q, k_arr, v_arr]
    out_specs = [q_spec]
    out_shape = [jax.ShapeDtypeStruct((bsz, n, NSA_HEADS * HEAD_DIM), F32)]
    if mode == "sel":
        n_blk = sel.shape[-1]
        in_specs.append(pl.BlockSpec((1, 1, tq, n_blk), lambda bi, gi, qi: (bi, gi, qi, 0)))
        args.append(sel)
    if mode == "cmp":
        out_specs.append(pl.BlockSpec((1, 1, tq, ATTN_TILE), lambda bi, gi, qi: (bi, gi, qi, 0)))
        out_shape.append(jax.ShapeDtypeStruct((bsz, NSA_KV, n, ATTN_TILE), F32))
    out = pl.pallas_call(
        functools.partial(_attn_kernel, mode=mode, n_keys=n_keys),
        grid=(bsz, NSA_KV, n // tq),
        in_specs=in_specs,
        out_specs=out_specs,
        out_shape=out_shape,
        compiler_params=pltpu.CompilerParams(
            dimension_semantics=("arbitrary", "arbitrary", "arbitrary"),
            vmem_limit_bytes=VMEM_LIMIT_BYTES),
    )(*args)
    return out if mode == "cmp" else out[0]


def _rms_norm(x, g):
    xf = x.astype(F32)
    y = xf * lax.rsqrt(jnp.mean(xf * xf, axis=-1, keepdims=True) + NORM_EPS)
    return (y * g.astype(F32)).astype(x.dtype)


def _masked_softmax(s, mask):
    s = jnp.where(mask, s.astype(F32), NEG)
    p = jnp.where(mask, jnp.exp(s - jnp.max(s, axis=-1, keepdims=True)), 0.0)
    return p / jnp.maximum(jnp.sum(p, axis=-1, keepdims=True), TINY)


def _partial_rope(x, pos):
    half = ROT_DIM // 2
    inv_freq = jnp.power(ROPE_THETA, -jnp.arange(half, dtype=F32) / half)
    ang = pos.astype(F32)[:, None] * inv_freq[None, :]
    cos = jnp.cos(ang)[:, None, :]
    sin = jnp.sin(ang)[:, None, :]
    xr = x[..., :ROT_DIM].astype(F32)
    x1, x2 = xr[..., :half], xr[..., half:]
    rot = jnp.concatenate([x1 * cos - x2 * sin, x2 * cos + x1 * sin], axis=-1)
    return jnp.concatenate([rot.astype(x.dtype), x[..., ROT_DIM:]], axis=-1)


def _causal_dwconv(x, buf, w, b):
    width = w.shape[0]
    n = x.shape[1]
    xx = jnp.concatenate([buf.astype(x.dtype), x], axis=1)
    y = b + sum(xx[:, j:j + n] * w[j] for j in range(width))
    return y, xx[:, xx.shape[1] - (width - 1):]


def _rglru_mixer(x_in, gate_in, conv_buf, h0, lp):
    b, n, _ = x_in.shape
    xc, conv_new = _causal_dwconv(x_in, conv_buf, lp["lru_conv_w"], lp["lru_conv_b"])
    xh = xc.reshape(b, n, LRU_HEADS, LRU_HD)
    r = jax.nn.sigmoid((jnp.einsum("bnhi,hij->bnhj", xh, lp["lru_wa"]).reshape(b, n, BR_W)
                        + lp["lru_ba"]).astype(F32))
    i = jax.nn.sigmoid((jnp.einsum("bnhi,hij->bnhj", xh, lp["lru_wx"]).reshape(b, n, BR_W)
                        + lp["lru_bx"]).astype(F32))
    log_a = -LRU_C * r * jax.nn.softplus(-lp["lru_lambda"].astype(F32))
    a = jnp.exp(log_a)
    u = jnp.sqrt(-jnp.expm1(2.0 * log_a)) * (i * xc.astype(F32))
    h = _lru_scan(a, u, h0.astype(F32))
    y = (h * jax.nn.gelu(gate_in.astype(F32))).astype(x_in.dtype)
    return y, conv_new, h[:, -1].astype(x_in.dtype)


def _compress_blocks(rows, pe, w1, w2):
    b, length, g, d = rows.shape
    chunks = rows.reshape(b, length // CMP_STRIDE, CMP_STRIDE, g, d)
    w1r = w1.reshape(CMP_LEN, d, CMP_HIDDEN)
    first = jnp.einsum("bcpgd,pde->bcge", chunks, w1r[:CMP_STRIDE])
    second = jnp.einsum("bcpgd,pde->bcge", chunks, w1r[CMP_STRIDE:])
    pe_term = jnp.einsum("pd,pde->e", pe, w1r)
    hidden = jax.nn.gelu(first[:, :-1] + second[:, 1:] + pe_term)
    return jnp.einsum("bcge,ed->bcgd", hidden, w2)


def _nsa_mixer_prompt(q, nsa_rows, win_rows, gates, lp):
    b, n = q.shape[0], q.shape[1]
    tq = jnp.arange(n)
    q2 = q.reshape(b, n, NSA_HEADS * HEAD_DIM)
    rows2 = nsa_rows.reshape(b, n, 4 * NSA_KV * HEAD_DIM)
    win2 = win_rows.reshape(b, n, 2 * NSA_KV * HEAD_DIM)
    k_cmp = _compress_blocks(nsa_rows[:, :, 0], lp["cmp_pe"][0], lp["cmp_w1"][0], lp["cmp_w2"][0])
    v_cmp = _compress_blocks(nsa_rows[:, :, 1], lp["cmp_pe"][1], lp["cmp_w1"][1], lp["cmp_w2"][1])
    n_cmp = k_cmp.shape[1]
    assert n_cmp <= ATTN_TILE
    cmp_end = jnp.arange(n_cmp) * CMP_STRIDE + (CMP_LEN - 1)
    k_cmp = _partial_rope(_rms_norm(k_cmp, lp["qk_norm"][1]), cmp_end)
    pad = ((0, 0), (0, ATTN_TILE - n_cmp), (0, 0))
    k_cmp2 = jnp.pad(k_cmp.reshape(b, n_cmp, NSA_KV * HEAD_DIM), pad)
    v_cmp2 = jnp.pad(v_cmp.reshape(b, n_cmp, NSA_KV * HEAD_DIM), pad)
    o_c, psum = _attn("cmp", q2, k_cmp2, lambda g: g, v_cmp2, lambda g: g, n_keys=n_cmp)
    n_slc = n // SEL_BLOCK
    ratio = SEL_BLOCK // CMP_STRIDE
    imp = jnp.pad(psum[..., :n_cmp], ((0, 0), (0, 0), (0, 0), (1, 1)))
    imp_slc = (imp[..., :ratio * n_slc].reshape(b, NSA_KV, n, n_slc, ratio).sum(-1)
               + imp[..., ratio::ratio])
    blk = jnp.arange(n_slc)[None, :]
    cur = (tq // SEL_BLOCK)[:, None]
    valid = blk * SEL_BLOCK <= tq[:, None]
    forced = (blk == 0) | (blk == cur) | (blk == cur - 1)
    score = jnp.where(valid, imp_slc + jnp.where(forced, FORCE_BONUS, 0.0), NEG)
    k_sel = min(N_SEL, n_slc)
    top_val, top_idx = lax.top_k(score, k_sel)
    top_ok = top_val > 0.5 * NEG
    sel = jnp.any((top_idx[..., None] == jnp.arange(n_slc)) & top_ok[..., None], axis=-2).astype(F32)
    o_s = _attn("sel", q2, rows2, lambda g: 2 * NSA_KV + g, rows2, lambda g: 3 * NSA_KV + g, sel=sel)
    o_w = _attn("win", q2, win2, lambda g: g, win2, lambda g: NSA_KV + g)
    g4 = gates.astype(q.dtype)
    o4 = lambda o: o.reshape(b, n, NSA_HEADS, HEAD_DIM)
    o = g4[..., 0:1] * o4(o_c) + g4[..., 1:2] * o4(o_s) + g4[..., 2:3] * o4(o_w)
    return o.reshape(b, n, NSA_HEADS * HEAD_DIM)


def _nsa_mixer_paged(q, nsa_rows, win_all, gates, cache, layer, page_table, lp):
    b, n = q.shape[0], q.shape[1]
    past_len = page_table.shape[1] * PAGE_SIZE
    assert n < CMP_STRIDE and n <= SEL_BLOCK and win_all.shape[1] == WINDOW + n
    scale = HEAD_DIM ** -0.5
    tq = past_len + jnp.arange(n)
    q5 = q.reshape(b, n, NSA_KV, NSA_HG, HEAD_DIM)
    first, second = _compress_pages(cache, layer, page_table, lp["cmp_w1"])
    w1r = lp["cmp_w1"].reshape(2, CMP_LEN, HEAD_DIM, CMP_HIDDEN)
    pe_term = jnp.einsum("kpd,kpde->ke", lp["cmp_pe"], w1r)
    hidden = jax.nn.gelu(first[:, :, :, :-1] + second[:, :, :, 1:] + pe_term[None, :, None, None, :])
    kv_cmp = jnp.einsum("bkgce,ked->bkgcd", hidden, lp["cmp_w2"])
    n_cmp = kv_cmp.shape[3]
    cmp_end = jnp.arange(n_cmp) * CMP_STRIDE + (CMP_LEN - 1)
    k_cmp = _partial_rope(_rms_norm(jnp.moveaxis(kv_cmp[:, 0], 1, 2), lp["qk_norm"][1]), cmp_end)
    v_cmp = jnp.moveaxis(kv_cmp[:, 1], 1, 2)
    s_c = jnp.einsum("bqghd,bkgd->bghqk", q5, k_cmp) * scale
    p_c = _masked_softmax(s_c, cmp_end[None, :] <= tq[:, None])
    o_c = jnp.einsum("bghqk,bkgd->bqghd", p_c.astype(q.dtype), v_cmp)
    n_slc = -(-(past_len + n) // SEL_BLOCK)
    ratio = SEL_BLOCK // CMP_STRIDE
    imp = jnp.pad(p_c.sum(axis=2), ((0, 0), (0, 0), (0, 0), (1, ratio * n_slc - n_cmp)))
    imp_slc = (imp[..., :ratio * n_slc].reshape(b, NSA_KV, n, n_slc, ratio).sum(-1)
               + imp[..., ratio::ratio])
    blk = jnp.arange(n_slc)[None, :]
    cur = (tq // SEL_BLOCK)[:, None]
    valid = blk * SEL_BLOCK <= tq[:, None]
    forced = (blk == 0) | (blk == cur) | (blk == cur - 1)
    score = jnp.where(valid, imp_slc + jnp.where(forced, FORCE_BONUS, 0.0), NEG)
    k_sel = min(N_SEL, n_slc)
    top_val, top_idx = lax.top_k(score, k_sel)
    top_ok = top_val > 0.5 * NEG
    n_past_blk = past_len // SEL_BLOCK
    per_page = PAGE_SIZE // SEL_BLOCK
    idx_c = jnp.minimum(top_idx, n_past_blk - 1)
    bi = jnp.arange(b)[:, None, None, None]
    gi = jnp.arange(NSA_KV)[None, :, None, None]
    page = page_table[bi, idx_c // per_page]
    row0 = (idx_c % per_page) * SEL_BLOCK

    def take(kind):
        one = lambda pg, r0, g: lax.dynamic_slice(
            cache, (layer, pg, r0, kind, g, 0), (1, 1, SEL_BLOCK, 1, 1, HEAD_DIM)).reshape(SEL_BLOCK, HEAD_DIM)
        flat = jax.vmap(one)(page.reshape(-1), row0.reshape(-1), jnp.broadcast_to(gi, page.shape).reshape(-1))
        past_blk = flat.reshape(b, NSA_KV, n, k_sel, SEL_BLOCK, HEAD_DIM)
        tail = jnp.pad(jnp.moveaxis(nsa_rows[:, :, kind], 1, 2), ((0, 0), (0, 0), (0, SEL_BLOCK - n), (0, 0)))
        return jnp.where((top_idx >= n_past_blk)[..., None, None], tail[:, :, None, None], past_blk)

    m = k_sel * SEL_BLOCK
    kg = take(2).reshape(b, NSA_KV, n, m, HEAD_DIM)
    vg = take(3).reshape(b, NSA_KV, n, m, HEAD_DIM)
    s = jnp.einsum("bqghd,bgqmd->bghqm", q5, kg) * scale
    kpos = (top_idx[..., None] * SEL_BLOCK + jnp.arange(SEL_BLOCK)).reshape(b, NSA_KV, n, m)
    mask = (kpos <= tq[None, None, :, None]) & jnp.repeat(top_ok, SEL_BLOCK, axis=-1)
    o_s = jnp.einsum("bghqm,bgqmd->bqghd", _masked_softmax(s, mask[:, :, None]).astype(q.dtype), vg)
    kpos_w = past_len - WINDOW + jnp.arange(WINDOW + n)
    mask_w = ((kpos_w[None, :] <= tq[:, None]) & (kpos_w[None, :] >= tq[:, None] - WINDOW)
              & (kpos_w[None, :] >= 0))
    s_w = jnp.einsum("bqghd,bkgd->bghqk", q5, win_all[:, :, 0]) * scale
    o_w = jnp.einsum("bghqk,bkgd->bqghd", _masked_softmax(s_w, mask_w).astype(q.dtype), win_all[:, :, 1])
    g5 = gates.reshape(b, n, NSA_KV, NSA_HG, 3).astype(q.dtype)
    o = g5[..., 0:1] * o_c + g5[..., 1:2] * o_s + g5[..., 2:3] * o_w
    return o.reshape(b, n, NSA_HEADS * HEAD_DIM)


def _rwkv7_mixer(z, shift_buf, s0, lp):
    b, n, _ = z.shape
    prev = jnp.concatenate([shift_buf[:, None].astype(z.dtype), z[:, :-1]], axis=1)
    zm = z + (prev - z) * lp["rwkv_mu"]
    r, k, v, wd, ad, gd = jnp.split(zm, RWKV_OFFSETS, axis=-1)
    w_log = -jax.nn.softplus(-(lp["rwkv_w0"] + jnp.tanh(wd) @ lp["rwkv_w2"]).astype(F32)) - 0.5
    decay = jnp.exp(-jnp.exp(w_log))
    a = jax.nn.sigmoid((lp["rwkv_a0"] + ad @ lp["rwkv_a2"]).astype(F32))
    g = jax.nn.sigmoid(gd) @ lp["rwkv_g2"]
    hs = lambda t: t.astype(F32).reshape(b, n, RWKV_HEADS, RWKV_HD)
    kk = hs(k * lp["rwkv_kk"])
    kk = kk / jnp.maximum(jnp.sqrt(jnp.sum(kk * kk, axis=-1, keepdims=True)), 1e-12)
    k_h = hs(k.astype(F32) * (1.0 + (a - 1.0) * lp["rwkv_ka"].astype(F32)))
    r_h, v_h, a_h = hs(r), hs(v), hs(a)
    flat = lambda t: t.reshape(b, n, BR_W)
    y, s_fin = _rwkv_scan(r.astype(F32), decay, flat(k_h), v.astype(F32), flat(kk), flat(-(kk * a_h)),
                          s0.astype(F32))
    y = y.reshape(b, n, RWKV_HEADS, RWKV_HD)
    mu = jnp.mean(y, axis=-1, keepdims=True)
    var = jnp.mean((y - mu) ** 2, axis=-1, keepdims=True)
    y = ((y - mu) * lax.rsqrt(var + GN_EPS)).reshape(b, n, BR_W) * lp["rwkv_gn_g"] + lp["rwkv_gn_b"]
    bonus = jnp.sum(r_h * k_h * lp["rwkv_rk"].astype(F32), axis=-1, keepdims=True) * v_h
    y = y + bonus.reshape(b, n, BR_W)
    out = (y * g.astype(F32)).astype(z.dtype)
    return out, s_fin.astype(z.dtype), z[:, -1]


def _trunk_layer(x, lp, st, q_off, win_keep):
    b, n, _ = x.shape
    dt = x.dtype
    t = b * n
    xn = _rms_norm(x, lp["norm1"]).reshape(t, D_MODEL).astype(BF16)
    (lru_x, lru_gate, q, kv_cmp, kv_sel, kv_win, nsa_gate, rwkv_in,
     merge_gate) = [_proj(xn, w).reshape(b, n, w.shape[1]) for w in lp["w_in"]]
    y_a, lru_conv_new, lru_h_new = _rglru_mixer(lru_x, lru_gate, st["lru_conv"], st["lru_h"], lp)
    pos = q_off + jnp.arange(n)
    qn = lp["qk_norm"]
    q = _partial_rope(_rms_norm(q.reshape(b, n, NSA_HEADS, HEAD_DIM), qn[0]), pos)
    kv_cmp = kv_cmp.reshape(b, n, 2, NSA_KV, HEAD_DIM)
    kv_sel = kv_sel.reshape(b, n, 2, NSA_KV, HEAD_DIM)
    kv_win = kv_win.reshape(b, n, 2, NSA_KV, HEAD_DIM)
    k_sel = _partial_rope(_rms_norm(kv_sel[:, :, 0], qn[2]), pos)
    k_win = _partial_rope(_rms_norm(kv_win[:, :, 0], qn[3]), pos)
    nsa_rows = jnp.stack([kv_cmp[:, :, 0], kv_cmp[:, :, 1], k_sel, kv_sel[:, :, 1]], axis=2)
    win_rows = jnp.stack([k_win, kv_win[:, :, 1]], axis=2)
    win_all = jnp.concatenate([st["win"].astype(dt), win_rows], axis=1)
    win_new = win_all[:, win_all.shape[1] - win_keep:]
    nsa_gates = jax.nn.sigmoid(nsa_gate.reshape(b, n, NSA_HEADS, 3))
    if st["paged_past"] is None:
        assert q_off == 0 and st["win"].shape[1] == 0
        y_b = _nsa_mixer_prompt(q, nsa_rows, win_rows, nsa_gates, lp)
    else:
        cache, layer, page_table = st["paged_past"]
        assert q_off == page_table.shape[1] * PAGE_SIZE
        y_b = _nsa_mixer_paged(q, nsa_rows, win_all, nsa_gates, cache, layer, page_table, lp)
    y_c, rwkv_s_new, rwkv_shift_new = _rwkv7_mixer(rwkv_in, st["rwkv_shift"], st["rwkv_s"], lp)
    gate = jax.nn.sigmoid(merge_gate.reshape(b, n, 3, D_MODEL))
    mix = 0.0
    for bi, y in enumerate((y_a, y_b, y_c)):
        pb = _proj(y.reshape(t, BR_W), lp["w_branch"][bi]).reshape(b, n, D_MODEL)
        mix = mix + gate[:, :, bi] * pb
    x = x + _proj(mix.reshape(t, D_MODEL), lp["w_out"]).reshape(b, n, D_MODEL)
    xn2 = _rms_norm(x, lp["norm2"]).reshape(t, D_MODEL).astype(BF16)
    hg, hv = [_proj(xn2, w).reshape(b, n, D_FF) for w in lp["ffn_w_in"]]
    hc, ffn_conv_new = _causal_dwconv(hg, st["ffn_conv"], lp["ffn_conv_w"], lp["ffn_conv_b"])
    act = jax.nn.gelu(hc) * hv
    x = x + _proj(act.reshape(t, D_FF), lp["ffn_w_down"]).reshape(b, n, D_MODEL)
    new_state = {"nsa": nsa_rows, "win": win_new, "lru_h": lru_h_new.astype(dt),
                 "lru_conv": lru_conv_new.astype(dt), "rwkv_s": rwkv_s_new.astype(dt),
                 "rwkv_shift": rwkv_shift_new.astype(dt), "ffn_conv": ffn_conv_new.astype(dt)}
    return x, new_state


def kernel(x_prompt, x_sample, cache_nsa_kv, state_win_kv, state_lru_h, state_lru_conv,
           state_rwkv_s, state_rwkv_shift, state_ffn_conv, page_table,
           norm1_g, norm2_g, w_in, lru_conv_w, lru_conv_b, lru_wa, lru_ba, lru_wx, lru_bx,
           lru_lambda, qk_norm_g, cmp_pe, cmp_w1, cmp_w2, rwkv_mu, rwkv_w0, rwkv_w2,
           rwkv_a0, rwkv_a2, rwkv_g2, rwkv_kk, rwkv_ka, rwkv_rk, rwkv_gn_g, rwkv_gn_b,
           w_branch, w_out, ffn_w_in, ffn_conv_w, ffn_conv_b, ffn_w_down):
    dt = x_prompt.dtype
    bp, n_p, _ = x_prompt.shape
    bs = x_sample.shape[0]
    depth = w_in.shape[0]
    past_len = page_table.shape[1] * PAGE_SIZE
    xp, xs = x_prompt, x_sample
    new_p = {name: [] for name in STATE_NAMES}
    new_s = {name: [] for name in STATE_NAMES}
    for l in range(depth):
        in_bounds = (0,) + IN_OFFSETS + (D_IN,)
        w_in_segs = [w_in[l][:, a:z].astype(BF16) for a, z in zip(in_bounds[:-1], in_bounds[1:])]
        lp = {"norm1": norm1_g[l], "norm2": norm2_g[l], "w_in": w_in_segs,
              "lru_conv_w": lru_conv_w[l], "lru_conv_b": lru_conv_b[l], "lru_wa": lru_wa[l],
              "lru_ba": lru_ba[l], "lru_wx": lru_wx[l], "lru_bx": lru_bx[l],
              "lru_lambda": lru_lambda[l], "qk_norm": qk_norm_g[l], "cmp_pe": cmp_pe[l],
              "cmp_w1": cmp_w1[l], "cmp_w2": cmp_w2[l], "rwkv_mu": rwkv_mu[l],
              "rwkv_w0": rwkv_w0[l], "rwkv_w2": rwkv_w2[l], "rwkv_a0": rwkv_a0[l],
              "rwkv_a2": rwkv_a2[l], "rwkv_g2": rwkv_g2[l], "rwkv_kk": rwkv_kk[l],
              "rwkv_ka": rwkv_ka[l], "rwkv_rk": rwkv_rk[l], "rwkv_gn_g": rwkv_gn_g[l],
              "rwkv_gn_b": rwkv_gn_b[l], "w_branch": w_branch[l].astype(BF16),
              "w_out": w_out[l].astype(BF16), "ffn_w_in": [ffn_w_in[l][:, :D_FF].astype(BF16), ffn_w_in[l][:, D_FF:].astype(BF16)],
              "ffn_conv_w": ffn_conv_w[l], "ffn_conv_b": ffn_conv_b[l],
              "ffn_w_down": ffn_w_down[l].astype(BF16)}
        st_p = {"paged_past": None,
                "win": jnp.zeros((bp, 0, 2, NSA_KV, HEAD_DIM), dt),
                "lru_h": jnp.zeros((bp, BR_W), dt),
                "lru_conv": jnp.zeros((bp, LRU_CONV - 1, BR_W), dt),
                "rwkv_s": jnp.zeros((bp, RWKV_HEADS, RWKV_HD, RWKV_HD), dt),
                "rwkv_shift": jnp.zeros((bp, RWKV_IN), dt),
                "ffn_conv": jnp.zeros((bp, FFN_CONV - 1, D_FF), dt)}
        xp, out_p = _trunk_layer(xp, lp, st_p, 0, min(WINDOW, n_p))
        st_s = {"paged_past": (cache_nsa_kv, l, page_table), "win": state_win_kv[l], "lru_h": state_lru_h[l],
                "lru_conv": state_lru_conv[l], "rwkv_s": state_rwkv_s[l],
                "rwkv_shift": state_rwkv_shift[l], "ffn_conv": state_ffn_conv[l]}
        xs, out_s = _trunk_layer(xs, lp, st_s, past_len, state_win_kv.shape[2])
        for name in STATE_NAMES:
            new_p[name].append(out_p[name])
            new_s[name].append(out_s[name])
    sp = {name: jnp.stack(new_p[name], axis=0) for name in STATE_NAMES}
    ss = {name: jnp.stack(new_s[name], axis=0) for name in STATE_NAMES}
    return (xp, xs, sp["nsa"], sp["win"], sp["lru_h"], sp["lru_conv"], sp["rwkv_s"],
            sp["rwkv_shift"], sp["ffn_conv"], ss["nsa"], ss["win"], ss["lru_h"], ss["lru_conv"],
            ss["rwkv_s"], ss["rwkv_shift"], ss["ffn_conv"])
```

```python
import functools
import math

import jax
import jax.numpy as jnp
import numpy as np
from jax import lax
from jax.experimental import pallas as pl
from jax.experimental.pallas import tpu as pltpu

F32 = jnp.float32
BF16 = jnp.bfloat16

D_MODEL = 4096
PAGE_SIZE = 128
BR_W = D_MODEL // 2
LRU_HEADS = 16
LRU_HD = BR_W // LRU_HEADS
LRU_CONV = 4
LRU_C = 8.0
HEAD_DIM = 128
NSA_HEADS = BR_W // HEAD_DIM
NSA_KV = 4
NSA_HG = NSA_HEADS // NSA_KV
ROT_DIM = HEAD_DIM // 4
ROPE_THETA = 500000.0
CMP_LEN = 32
CMP_STRIDE = 16
CMP_HIDDEN = 2 * HEAD_DIM
SEL_BLOCK = 64
N_SEL = 16
WINDOW = 512
FORCE_BONUS = 1e4
NEG = -1e30
TINY = 1e-30
RWKV_HD = 64
RWKV_HEADS = BR_W // RWKV_HD
LORA_W = 64
LORA_A = 64
LORA_G = 128
RWKV_SIZES = (BR_W, BR_W, BR_W, LORA_W, LORA_A, LORA_G)
RWKV_IN = sum(RWKV_SIZES)
RWKV_OFFSETS = tuple(int(v) for v in np.cumsum(RWKV_SIZES)[:-1])
GN_EPS = 64e-5
D_FF = ((8 * D_MODEL // 3 + 255) // 256) * 256
FFN_CONV = 3
NORM_EPS = 1e-6
IN_SIZES = (BR_W, BR_W, NSA_HEADS * HEAD_DIM, 2 * NSA_KV * HEAD_DIM, 2 * NSA_KV * HEAD_DIM,
            2 * NSA_KV * HEAD_DIM, 3 * NSA_HEADS, RWKV_IN, 3 * D_MODEL)
D_IN = sum(IN_SIZES)
IN_OFFSETS = tuple(int(v) for v in np.cumsum(IN_SIZES)[:-1])
STATE_NAMES = ("nsa", "win", "lru_h", "lru_conv", "rwkv_s", "rwkv_shift", "ffn_conv")

VMEM_LIMIT_BYTES = 56 * 1024 * 1024


def _mm_kernel(x_ref, w_ref, o_ref, *, nk):
    part = jnp.dot(x_ref[...], w_ref[...], preferred_element_type=F32)
    if nk == 1:
        o_ref[...] = part
    else:
        k = pl.program_id(2)

        @pl.when(k == 0)
        def _():
            o_ref[...] = part

        @pl.when(k != 0)
        def _():
            o_ref[...] += part


def _matmul(x, w, *, tm, tn, tk=None):
    m, kdim = x.shape
    n = w.shape[1]
    tk = kdim if tk is None else tk
    assert m % tm == 0 and kdim % tk == 0
    nk = kdim // tk
    grid = (m // tm, pl.cdiv(n, tn), nk)
    return pl.pallas_call(
        functools.partial(_mm_kernel, nk=nk),
        grid=grid,
        in_specs=[pl.BlockSpec((tm, tk), lambda i, j, k: (i, k)),
                  pl.BlockSpec((tk, tn), lambda i, j, k: (k, j))],
        out_specs=pl.BlockSpec((tm, tn), lambda i, j, k: (i, j)),
        out_shape=jax.ShapeDtypeStruct((m, n), F32),
        compiler_params=pltpu.CompilerParams(
            dimension_semantics=("arbitrary", "arbitrary", "arbitrary"),
            vmem_limit_bytes=VMEM_LIMIT_BYTES),
    )(x, w)


def _proj(x2d, w_bf16):
    m, kdim = x2d.shape
    n = w_bf16.shape[1]
    xb = x2d.astype(BF16)
    tm, tn = (1024, 512) if m >= 1024 else (m, 1024)
    tn = min(tn, n)
    tk = kdim if kdim <= 4096 else kdim // 2
    return _matmul(xb, w_bf16, tm=tm, tn=tn, tk=tk)


RWKV_HEAD_BLOCK = 16
LANES = 128


def _rwkv_scan_kernel(r_ref, w_ref, k_ref, v_ref, kk_ref, b_ref, s0_ref, y_ref, s_ref, *, tt, hb):
    @pl.when(pl.program_id(2) == 0)
    def _():
        s_ref[...] = s0_ref[...]

    hd = RWKV_HD
    shift = hd.bit_length() - 1
    iota = lambda shape, d: lax.broadcasted_iota(jnp.int32, shape, d)
    eye2 = ((iota((hd, LANES), 1) & (hd - 1)) == iota((hd, LANES), 0)).astype(F32)
    blk_ones = (lax.shift_right_logical(iota((LANES, LANES), 0), shift)
                == lax.shift_right_logical(iota((LANES, LANES), 1), shift)).astype(BF16)
    sub = min(tt, 8)
    pairs = range(hb // 2)
    ps = [slice(p * hd, (p + 1) * hd) for p in pairs]

    def head_sums(x):
        return jnp.dot(x.astype(BF16), blk_ones, preferred_element_type=F32)

    def head_sums_f32(x):
        hi = x.astype(BF16)
        return jnp.dot(hi, blk_ones, preferred_element_type=F32) + head_sums(x - hi.astype(F32))

    def block(tb, carry):
        rows = pl.ds(pl.multiple_of(tb * sub, sub), sub)
        blk = [ref[0, rows, :] for ref in (kk_ref, w_ref, b_ref, k_ref, v_ref, r_ref)]
        vec = lambda q, i, p: blk[q][i:i + 1, p * LANES:(p + 1) * LANES]
        ys = [[] for _ in pairs]
        for i in range(sub):
            s = [s_ref[0, ps[p], :] for p in pairs]
            sa = head_sums_f32(jnp.concatenate([s[p] * vec(0, i, p) for p in pairs], axis=0))
            vcol = head_sums(jnp.concatenate([eye2 * vec(4, i, p) for p in pairs], axis=0))
            for p in pairs:
                s[p] = s[p] * vec(1, i, p) + sa[ps[p]] * vec(2, i, p) + vcol[ps[p]] * vec(3, i, p)
                s_ref[0, ps[p], :] = s[p]
            ycol = head_sums(jnp.concatenate([s[p] * vec(5, i, p) for p in pairs], axis=0))
            for p in pairs:
                ys[p].append(jnp.sum(ycol[ps[p]] * eye2, axis=0, keepdims=True))
        for p in pairs:
            y_ref[0, rows, p * LANES:(p + 1) * LANES] = jnp.concatenate(ys[p], axis=0)
        return carry

    lax.fori_loop(0, tt // sub, block, 0)


def _rwkv_scan(r, w, k, v, kk, b, s0):
    bsz, n, _ = r.shape
    hb, hd = RWKV_HEAD_BLOCK, RWKV_HD
    assert 2 * hd == LANES and RWKV_HEADS % hb == 0 and hb % 2 == 0
    tt = math.gcd(n, 256)
    n_pairs = RWKV_HEADS // 2
    vec = pl.BlockSpec((1, tt, hb * hd), lambda bi, hi, ti: (bi, ti, hi))
    st = pl.BlockSpec((1, hb // 2 * hd, LANES), lambda bi, hi, ti: (bi, hi, 0))
    s0p = s0.reshape(bsz, n_pairs, 2, hd, hd).transpose(0, 1, 3, 2, 4).reshape(bsz, n_pairs * hd, LANES)
    y, s_fin = pl.pallas_call(
        functools.partial(_rwkv_scan_kernel, tt=tt, hb=hb),
        grid=(bsz, RWKV_HEADS // hb, n // tt),
        in_specs=[vec] * 6 + [st],
        out_specs=[vec, st],
        out_shape=[jax.ShapeDtypeStruct((bsz, n, BR_W), F32),
                   jax.ShapeDtypeStruct((bsz, n_pairs * hd, LANES), F32)],
        compiler_params=pltpu.CompilerParams(
            dimension_semantics=("arbitrary", "arbitrary", "arbitrary"),
            vmem_limit_bytes=VMEM_LIMIT_BYTES),
    )(r, w, k, v, kk, b, s0p)
    s_fin = s_fin.reshape(bsz, n_pairs, hd, 2, hd).transpose(0, 1, 3, 2, 4)
    return y, s_fin.reshape(bsz, RWKV_HEADS, hd, hd)


def _lru_scan_kernel(a_ref, u_ref, h0_ref, h_ref, carry_ref, *, tt):
    @pl.when(pl.program_id(1) == 0)
    def _():
        carry_ref[...] = h0_ref[0]

    sub = min(tt, 8)

    def block(tb, h):
        rows = pl.ds(pl.multiple_of(tb * sub, sub), sub)
        a8, u8 = a_ref[0, rows, :], u_ref[0, rows, :]
        outs = []
        for i in range(sub):
            h = a8[i:i + 1] * h + u8[i:i + 1]
            outs.append(h)
        h_ref[0, rows, :] = jnp.concatenate(outs, axis=0)
        return h

    carry_ref[...] = lax.fori_loop(0, tt // sub, block, carry_ref[...])


def _lru_scan(a, u, h0):
    bsz, n, c = a.shape
    tt = math.gcd(n, 256)
    vec = pl.BlockSpec((1, tt, c), lambda bi, ti: (bi, ti, 0))
    return pl.pallas_call(
        functools.partial(_lru_scan_kernel, tt=tt),
        grid=(bsz, n // tt),
        in_specs=[vec, vec, pl.BlockSpec((1, 1, c), lambda bi, ti: (bi, 0, 0))],
        out_specs=vec,
        out_shape=jax.ShapeDtypeStruct((bsz, n, c), F32),
        scratch_shapes=[pltpu.VMEM((1, c), F32)],
        compiler_params=pltpu.CompilerParams(
            dimension_semantics=("arbitrary", "arbitrary"),
            vmem_limit_bytes=VMEM_LIMIT_BYTES),
    )(a, u, h0.reshape(bsz, 1, c))


CMP_PAGES_PER_STEP = 8
ROW_COMBOS = 4 * NSA_KV


def _compress_pages_kernel(pt_ref, *refs, pp):
    del pt_ref
    pages, (w1_ref, first_ref, second_ref) = refs[:pp], refs[pp:]
    cpp = PAGE_SIZE // CMP_STRIDE
    for kind in range(2):
        acc_f = jnp.zeros((NSA_KV * pp * cpp, CMP_HIDDEN), F32)
        acc_s = jnp.zeros((NSA_KV * pp * cpp, CMP_HIDDEN), F32)
        for p in range(CMP_STRIDE):
            x = jnp.concatenate(
                [page[pl.ds(p * ROW_COMBOS + kind * NSA_KV + g, cpp, stride=CMP_STRIDE * ROW_COMBOS), :]
                 for g in range(NSA_KV) for page in pages], axis=0).astype(BF16)
            acc_f += jnp.dot(x, w1_ref[kind, p], preferred_element_type=F32)
            acc_s += jnp.dot(x, w1_ref[kind, CMP_STRIDE + p], preferred_element_type=F32)
        for g in range(NSA_KV):
            first_ref[0, kind, g] = acc_f[g * pp * cpp:(g + 1) * pp * cpp]
            second_ref[0, kind, g] = acc_s[g * pp * cpp:(g + 1) * pp * cpp]


def _compress_pages(cache, layer, page_table, w1):
    depth, n_pool = cache.shape[0], cache.shape[1]
    bsz, n_pages = page_table.shape
    pp = math.gcd(n_pages, CMP_PAGES_PER_STEP)
    cpp = PAGE_SIZE // CMP_STRIDE
    cache2 = cache.reshape(depth, n_pool, PAGE_SIZE * ROW_COMBOS, HEAD_DIM)
    w1r = w1.reshape(2, CMP_LEN, HEAD_DIM, CMP_HIDDEN).astype(BF16)

    def page_spec(k):
        return pl.BlockSpec((None, None, PAGE_SIZE * ROW_COMBOS, HEAD_DIM),
                            lambda bi, pi, pt: (layer, pt[bi, pi * pp + k], 0, 0))

    out_spec = pl.BlockSpec((1, 2, NSA_KV, pp * cpp, CMP_HIDDEN), lambda bi, pi, pt: (bi, 0, 0, pi, 0))
    out_sds = jax.ShapeDtypeStruct((bsz, 2, NSA_KV, n_pages * cpp, CMP_HIDDEN), F32)
    return pl.pallas_call(
        functools.partial(_compress_pages_kernel, pp=pp),
        grid_spec=pltpu.PrefetchScalarGridSpec(
            num_scalar_prefetch=1,
            grid=(bsz, n_pages // pp),
            in_specs=[page_spec(k) for k in range(pp)]
            + [pl.BlockSpec((2, CMP_LEN, HEAD_DIM, CMP_HIDDEN), lambda bi, pi, pt: (0, 0, 0, 0))],
            out_specs=[out_spec, out_spec]),
        out_shape=[out_sds, out_sds],
        compiler_params=pltpu.CompilerParams(
            dimension_semantics=("arbitrary", "arbitrary"),
            vmem_limit_bytes=VMEM_LIMIT_BYTES),
    )(page_table, *([cache2] * pp), w1r)


ATTN_TILE = 128


def _attn_kernel(*refs, mode, n_keys):
    if mode == "cmp":
        q_ref, k_ref, v_ref, o_ref, p_ref = refs
    elif mode == "sel":
        q_ref, k_ref, v_ref, sel_ref, o_ref = refs
    else:
        q_ref, k_ref, v_ref, o_ref = refs
    tq = tk = ATTN_TILE
    qi = pl.program_id(2)
    scale = HEAD_DIM ** -0.5
    q = q_ref[0]
    qs = jnp.concatenate([q[:, h * HEAD_DIM:(h + 1) * HEAD_DIM] for h in range(NSA_HG)], axis=0)
    mm_dt, mm_prec = (F32, lax.Precision.HIGHEST) if mode == "cmp" else (BF16, None)
    qs = qs.astype(mm_dt)
    tpos = qi * tq + lax.broadcasted_iota(jnp.int32, (NSA_HG * tq, tk), 0) % tq
    kiota = lax.broadcasted_iota(jnp.int32, (NSA_HG * tq, tk), 1)

    def scores(kj):
        k_t = k_ref[0, pl.ds(pl.multiple_of(kj * tk, tk), tk), :].astype(mm_dt)
        s = lax.dot_general(qs, k_t, (((1,), (1,)), ((), ())), precision=mm_prec,
                            preferred_element_type=F32) * scale
        return s, kiota + kj * tk

    def finish(acc, l):
        o = acc / jnp.maximum(l, TINY)
        o_ref[0] = jnp.concatenate([o[h * tq:(h + 1) * tq] for h in range(NSA_HG)], axis=1)

    if mode == "cmp":
        s, kidx = scores(0)
        mask = (kidx * CMP_STRIDE + (CMP_LEN - 1) <= tpos) & (kidx < n_keys)
        s = jnp.where(mask, s, NEG)
        p = jnp.where(mask, jnp.exp(s - jnp.max(s, axis=-1, keepdims=True)), 0.0)
        p = p / jnp.maximum(jnp.sum(p, axis=-1, keepdims=True), TINY)
        p_ref[0, 0] = sum(p[h * tq:(h + 1) * tq] for h in range(NSA_HG))
        o = jnp.dot(p.astype(BF16), v_ref[0].astype(BF16), preferred_element_type=F32)
        o_ref[0] = jnp.concatenate([o[h * tq:(h + 1) * tq] for h in range(NSA_HG)], axis=1)
        return

    if mode == "sel":
        sel = sel_ref[0, 0].astype(BF16)
        n_blk = sel.shape[1]
        blk_row = lax.broadcasted_iota(jnp.int32, (n_blk, tk), 0)
        blk_col = lax.broadcasted_iota(jnp.int32, (n_blk, tk), 1)
        lo = 0
    else:
        lo = jnp.maximum(qi - WINDOW // tk, 0)

    def body(kj, carry):
        m, l, acc = carry
        s, kpos = scores(kj)
        mask = kpos <= tpos
        if mode == "sel":
            expand = jnp.where((blk_col + kj * tk) // SEL_BLOCK == blk_row, 1.0, 0.0).astype(BF16)
            picked = jnp.dot(sel, expand, preferred_element_type=F32) > 0.5
            mask = mask & jnp.concatenate([picked] * NSA_HG, axis=0)
        else:
            mask = mask & (kpos >= tpos - WINDOW)
        s = jnp.where(mask, s, NEG)
        m_new = jnp.maximum(m, jnp.max(s, axis=-1, keepdims=True))
        alpha = jnp.exp(m - m_new)
        p = jnp.where(mask, jnp.exp(s - m_new), 0.0)
        l = alpha * l + jnp.sum(p, axis=-1, keepdims=True)
        v_t = v_ref[0, pl.ds(pl.multiple_of(kj * tk, tk), tk), :].astype(BF16)
        acc = alpha * acc + jnp.dot(p.astype(BF16), v_t, preferred_element_type=F32)
        return m_new, l, acc

    rows = NSA_HG * tq
    init = (jnp.full((rows, 1), NEG, F32), jnp.zeros((rows, 1), F32), jnp.zeros((rows, HEAD_DIM), F32))
    _, l, acc = lax.fori_loop(lo, qi + 1, body, init)
    finish(acc, l)


def _attn(mode, q, k_arr, k_lane_block, v_arr, v_lane_block, sel=None, n_keys=None):
    bsz, n, _ = q.shape
    length = k_arr.shape[1]
    tq = ATTN_TILE
    assert n % tq == 0 and length % ATTN_TILE == 0
    gw = NSA_HG * HEAD_DIM
    q_spec = pl.BlockSpec((1, tq, gw), lambda bi, gi, qi: (bi, qi, gi))
    k_spec = pl.BlockSpec((1, length, HEAD_DIM), lambda bi, gi, qi: (bi, 0, k_lane_block(gi)))
    v_spec = pl.BlockSpec((1, length, HEAD_DIM), lambda bi, gi, qi: (bi, 0, v_lane_block(gi)))
    in_specs = [q_spec, k_spec, v_spec]
    args = [q, k_arr, v_arr]
    out_specs = [q_spec]
    out_shape = [jax.ShapeDtypeStruct((bsz, n, NSA_HEADS * HEAD_DIM), F32)]
    if mode == "sel":
        n_blk = sel.shape[-1]
        in_specs.append(pl.BlockSpec((1, 1, tq, n_blk), lambda bi, gi, qi: (bi, gi, qi, 0)))
        args.append(sel)
    if mode == "cmp":
        out_specs.append(pl.BlockSpec((1, 1, tq, ATTN_TILE), lambda bi, gi, qi: (bi, gi, qi, 0)))
        out_shape.append(jax.ShapeDtypeStruct((bsz, NSA_KV, n, ATTN_TILE), F32))
    out = pl.pallas_call(
        functools.partial(_attn_kernel, mode=mode, n_keys=n_keys),
        grid=(bsz, NSA_KV, n // tq),
        in_specs=in_specs,
        out_specs=out_specs,
        out_shape=out_shape,
        compiler_params=pltpu.CompilerParams(
            dimension_semantics=("arbitrary", "arbitrary", "arbitrary"),
            vmem_limit_bytes=VMEM_LIMIT_BYTES),
    )(*args)
    return out if mode == "cmp" else out[0]


def _rms_norm(x, g):
    xf = x.astype(F32)
    y = xf * lax.rsqrt(jnp.mean(xf * xf, axis=-1, keepdims=True) + NORM_EPS)
    return (y * g.astype(F32)).astype(x.dtype)


def _masked_softmax(s, mask):
    s = jnp.where(mask, s.astype(F32), NEG)
    p = jnp.where(mask, jnp.exp(s - jnp.max(s, axis=-1, keepdims=True)), 0.0)
    return p / jnp.maximum(jnp.sum(p, axis=-1, keepdims=True), TINY)


def _partial_rope(x, pos):
    half = ROT_DIM // 2
    inv_freq = jnp.power(ROPE_THETA, -jnp.arange(half, dtype=F32) / half)
    ang = pos.astype(F32)[:, None] * inv_freq[None, :]
    cos = jnp.cos(ang)[:, None, :]
    sin = jnp.sin(ang)[:, None, :]
    xr = x[..., :ROT_DIM].astype(F32)
    x1, x2 = xr[..., :half], xr[..., half:]
    rot = jnp.concatenate([x1 * cos - x2 * sin, x2 * cos + x1 * sin], axis=-1)
    return jnp.concatenate([rot.astype(x.dtype), x[..., ROT_DIM:]], axis=-1)


def _causal_dwconv(x, buf, w, b):
    width = w.shape[0]
    n = x.shape[1]
    xx = jnp.concatenate([buf.astype(x.dtype), x], axis=1)
    y = b + sum(xx[:, j:j + n] * w[j] for j in range(width))
    return y, xx[:, xx.shape[1] - (width - 1):]


def _rglru_mixer(x_in, gate_in, conv_buf, h0, lp):
    b, n, _ = x_in.shape
    xc, conv_new = _causal_dwconv(x_in, conv_buf, lp["lru_conv_w"], lp["lru_conv_b"])
    xh = xc.reshape(b, n, LRU_HEADS, LRU_HD)
    r = jax.nn.sigmoid((jnp.einsum("bnhi,hij->bnhj", xh, lp["lru_wa"]).reshape(b, n, BR_W)
                        + lp["lru_ba"]).astype(F32))
    i = jax.nn.sigmoid((jnp.einsum("bnhi,hij->bnhj", xh, lp["lru_wx"]).reshape(b, n, BR_W)
                        + lp["lru_bx"]).astype(F32))
    log_a = -LRU_C * r * jax.nn.softplus(-lp["lru_lambda"].astype(F32))
    a = jnp.exp(log_a)
    u = jnp.sqrt(-jnp.expm1(2.0 * log_a)) * (i * xc.astype(F32))
    h = _lru_scan(a, u, h0.astype(F32))
    y = (h * jax.nn.gelu(gate_in.astype(F32))).astype(x_in.dtype)
    return y, conv_new, h[:, -1].astype(x_in.dtype)


def _compress_blocks(rows, pe, w1, w2):
    b, length, g, d = rows.shape
    chunks = rows.reshape(b, length // CMP_STRIDE, CMP_STRIDE, g, d)
    w1r = w1.reshape(CMP_LEN, d, CMP_HIDDEN)
    first = jnp.einsum("bcpgd,pde->bcge", chunks, w1r[:CMP_STRIDE])
    second = jnp.einsum("bcpgd,pde->bcge", chunks, w1r[CMP_STRIDE:])
    pe_term = jnp.einsum("pd,pde->e", pe, w1r)
    hidden = jax.nn.gelu(first[:, :-1] + second[:, 1:] + pe_term)
    return jnp.einsum("bcge,ed->bcgd", hidden, w2)


def _nsa_mixer_prompt(q, nsa_rows, win_rows, gates, lp):
    b, n = q.shape[0], q.shape[1]
    tq = jnp.arange(n)
    q2 = q.reshape(b, n, NSA_HEADS * HEAD_DIM)
    rows2 = nsa_rows.reshape(b, n, 4 * NSA_KV * HEAD_DIM)
    win2 = win_rows.reshape(b, n, 2 * NSA_KV * HEAD_DIM)
    k_cmp = _compress_blocks(nsa_rows[:, :, 0], lp["cmp_pe"][0], lp["cmp_w1"][0], lp["cmp_w2"][0])
    v_cmp = _compress_blocks(nsa_rows[:, :, 1], lp["cmp_pe"][1], lp["cmp_w1"][1], lp["cmp_w2"][1])
    n_cmp = k_cmp.shape[1]
    assert n_cmp <= ATTN_TILE
    cmp_end = jnp.arange(n_cmp) * CMP_STRIDE + (CMP_LEN - 1)
    k_cmp = _partial_rope(_rms_norm(k_cmp, lp["qk_norm"][1]), cmp_end)
    pad = ((0, 0), (0, ATTN_TILE - n_cmp), (0, 0))
    k_cmp2 = jnp.pad(k_cmp.reshape(b, n_cmp, NSA_KV * HEAD_DIM), pad)
    v_cmp2 = jnp.pad(v_cmp.reshape(b, n_cmp, NSA_KV * HEAD_DIM), pad)
    o_c, psum = _attn("cmp", q2, k_cmp2, lambda g: g, v_cmp2, lambda g: g, n_keys=n_cmp)
    n_slc = n // SEL_BLOCK
    ratio = SEL_BLOCK // CMP_STRIDE
    imp = jnp.pad(psum[..., :n_cmp], ((0, 0), (0, 0), (0, 0), (1, 1)))
    imp_slc = (imp[..., :ratio * n_slc].reshape(b, NSA_KV, n, n_slc, ratio).sum(-1)
               + imp[..., ratio::ratio])
    blk = jnp.arange(n_slc)[None, :]
    cur = (tq // SEL_BLOCK)[:, None]
    valid = blk * SEL_BLOCK <= tq[:, None]
    forced = (blk == 0) | (blk == cur) | (blk == cur - 1)
    score = jnp.where(valid, imp_slc + jnp.where(forced, FORCE_BONUS, 0.0), NEG)
    k_sel = min(N_SEL, n_slc)
    top_val, top_idx = lax.top_k(score, k_sel)
    top_ok = top_val > 0.5 * NEG
    sel = jnp.any((top_idx[..., None] == jnp.arange(n_slc)) & top_ok[..., None], axis=-2).astype(F32)
    o_s = _attn("sel", q2, rows2, lambda g: 2 * NSA_KV + g, rows2, lambda g: 3 * NSA_KV + g, sel=sel)
    o_w = _attn("win", q2, win2, lambda g: g, win2, lambda g: NSA_KV + g)
    g4 = gates.astype(q.dtype)
    o4 = lambda o: o.reshape(b, n, NSA_HEADS, HEAD_DIM)
    o = g4[..., 0:1] * o4(o_c) + g4[..., 1:2] * o4(o_s) + g4[..., 2:3] * o4(o_w)
    return o.reshape(b, n, NSA_HEADS * HEAD_DIM)


def _nsa_mixer_paged(q, nsa_rows, win_all, gates, cache, layer, page_table, lp):
    b, n = q.shape[0], q.shape[1]
    past_len = page_table.shape[1] * PAGE_SIZE
    assert n < CMP_STRIDE and n <= SEL_BLOCK and win_all.shape[1] == WINDOW + n
    scale = HEAD_DIM ** -0.5
    tq = past_len + jnp.arange(n)
    q5 = q.reshape(b, n, NSA_KV, NSA_HG, HEAD_DIM)
    first, second = _compress_pages(cache, layer, page_table, lp["cmp_w1"])
    w1r = lp["cmp_w1"].reshape(2, CMP_LEN, HEAD_DIM, CMP_HIDDEN)
    pe_term = jnp.einsum("kpd,kpde->ke", lp["cmp_pe"], w1r)
    hidden = jax.nn.gelu(first[:, :, :, :-1] + second[:, :, :, 1:] + pe_term[None, :, None, None, :])
    kv_cmp = jnp.einsum("bkgce,ked->bkgcd", hidden, lp["cmp_w2"])
    n_cmp = kv_cmp.shape[3]
    cmp_end = jnp.arange(n_cmp) * CMP_STRIDE + (CMP_LEN - 1)
    k_cmp = _partial_rope(_rms_norm(jnp.moveaxis(kv_cmp[:, 0], 1, 2), lp["qk_norm"][1]), cmp_end)
    v_cmp = jnp.moveaxis(kv_cmp[:, 1], 1, 2)
    s_c = jnp.einsum("bqghd,bkgd->bghqk", q5, k_cmp) * scale
    p_c = _masked_softmax(s_c, cmp_end[None, :] <= tq[:, None])
    o_c = jnp.einsum("bghqk,bkgd->bqghd", p_c.astype(q.dtype), v_cmp)
    n_slc = -(-(past_len + n) // SEL_BLOCK)
    ratio = SEL_BLOCK // CMP_STRIDE
    imp = jnp.pad(p_c.sum(axis=2), ((0, 0), (0, 0), (0, 0), (1, ratio * n_slc - n_cmp)))
    imp_slc = (imp[..., :ratio * n_slc].reshape(b, NSA_KV, n, n_slc, ratio).sum(-1)
               + imp[..., ratio::ratio])
    blk = jnp.arange(n_slc)[None, :]
    cur = (tq // SEL_BLOCK)[:, None]
    valid = blk * SEL_BLOCK <= tq[:, None]
    forced = (blk == 0) | (blk == cur) | (blk == cur - 1)
    score = jnp.where(valid, imp_slc + jnp.where(forced, FORCE_BONUS, 0.0), NEG)
    k_sel = min(N_SEL, n_slc)
    top_val, top_idx = lax.top_k(score, k_sel)
    top_ok = top_val > 0.5 * NEG
    n_past_blk = past_len // SEL_BLOCK
    per_page = PAGE_SIZE // SEL_BLOCK
    idx_c = jnp.minimum(top_idx, n_past_blk - 1)
    bi = jnp.arange(b)[:, None, None, None]
    gi = jnp.arange(NSA_KV)[None, :, None, None]
    page = page_table[bi, idx_c // per_page]
    row0 = (idx_c % per_page) * SEL_BLOCK

    def take(kind):
        one = lambda pg, r0, g: lax.dynamic_slice(
            cache, (layer, pg, r0, kind, g, 0), (1, 1, SEL_BLOCK, 1, 1, HEAD_DIM)).reshape(SEL_BLOCK, HEAD_DIM)
        flat = jax.vmap(one)(page.reshape(-1), row0.reshape(-1), jnp.broadcast_to(gi, page.shape).reshape(-1))
        past_blk = flat.reshape(b, NSA_KV, n, k_sel, SEL_BLOCK, HEAD_DIM)
        tail = jnp.pad(jnp.moveaxis(nsa_rows[:, :, kind], 1, 2), ((0, 0), (0, 0), (0, SEL_BLOCK - n), (0, 0)))
        return jnp.where((top_idx >= n_past_blk)[..., None, None], tail[:, :, None, None], past_blk)

    m = k_sel * SEL_BLOCK
    kg = take(2).reshape(b, NSA_KV, n, m, HEAD_DIM)
    vg = take(3).reshape(b, NSA_KV, n, m, HEAD_DIM)
    s = jnp.einsum("bqghd,bgqmd->bghqm", q5, kg) * scale
    kpos = (top_idx[..., None] * SEL_BLOCK + jnp.arange(SEL_BLOCK)).reshape(b, NSA_KV, n, m)
    mask = (kpos <= tq[None, None, :, None]) & jnp.repeat(top_ok, SEL_BLOCK, axis=-1)
    o_s = jnp.einsum("bghqm,bgqmd->bqghd", _masked_softmax(s, mask[:, :, None]).astype(q.dtype), vg)
    kpos_w = past_len - WINDOW + jnp.arange(WINDOW + n)
    mask_w = ((kpos_w[None, :] <= tq[:, None]) & (kpos_w[None, :] >= tq[:, None] - WINDOW)
              & (kpos_w[None, :] >= 0))
    s_w = jnp.einsum("bqghd,bkgd->bghqk", q5, win_all[:, :, 0]) * scale
    o_w = jnp.einsum("bghqk,bkgd->bqghd", _masked_softmax(s_w, mask_w).astype(q.dtype), win_all[:, :, 1])
    g5 = gates.reshape(b, n, NSA_KV, NSA_HG, 3).astype(q.dtype)
    o = g5[..., 0:1] * o_c + g5[..., 1:2] * o_s + g5[..., 2:3] * o_w
    return o.reshape(b, n, NSA_HEADS * HEAD_DIM)


def _rwkv7_mixer(z, shift_buf, s0, lp):
    b, n, _ = z.shape
    prev = jnp.concatenate([shift_buf[:, None].astype(z.dtype), z[:, :-1]], axis=1)
    zm = z + (prev - z) * lp["rwkv_mu"]
    r, k, v, wd, ad, gd = jnp.split(zm, RWKV_OFFSETS, axis=-1)
    w_log = -jax.nn.softplus(-(lp["rwkv_w0"] + jnp.tanh(wd) @ lp["rwkv_w2"]).astype(F32)) - 0.5
    decay = jnp.exp(-jnp.exp(w_log))
    a = jax.nn.sigmoid((lp["rwkv_a0"] + ad @ lp["rwkv_a2"]).astype(F32))
    g = jax.nn.sigmoid(gd) @ lp["rwkv_g2"]
    hs = lambda t: t.astype(F32).reshape(b, n, RWKV_HEADS, RWKV_HD)
    kk = hs(k * lp["rwkv_kk"])
    kk = kk / jnp.maximum(jnp.sqrt(jnp.sum(kk * kk, axis=-1, keepdims=True)), 1e-12)
    k_h = hs(k.astype(F32) * (1.0 + (a - 1.0) * lp["rwkv_ka"].astype(F32)))
    r_h, v_h, a_h = hs(r), hs(v), hs(a)
    flat = lambda t: t.reshape(b, n, BR_W)
    y, s_fin = _rwkv_scan(r.astype(F32), decay, flat(k_h), v.astype(F32), flat(kk), flat(-(kk * a_h)),
                          s0.astype(F32))
    y = y.reshape(b, n, RWKV_HEADS, RWKV_HD)
    mu = jnp.mean(y, axis=-1, keepdims=True)
    var = jnp.mean((y - mu) ** 2, axis=-1, keepdims=True)
    y = ((y - mu) * lax.rsqrt(var + GN_EPS)).reshape(b, n, BR_W) * lp["rwkv_gn_g"] + lp["rwkv_gn_b"]
    bonus = jnp.sum(r_h * k_h * lp["rwkv_rk"].astype(F32), axis=-1, keepdims=True) * v_h
    y = y + bonus.reshape(b, n, BR_W)
    out = (y * g.astype(F32)).astype(z.dtype)
    return out, s_fin.astype(z.dtype), z[:, -1]


def _trunk_layer(x, lp, st, q_off, win_keep):
    b, n, _ = x.shape
    dt = x.dtype
    t = b * n
    xn = _rms_norm(x, lp["norm1"]).reshape(t, D_MODEL).astype(BF16)
    (lru_x, lru_gate, q, kv_cmp, kv_sel, kv_win, nsa_gate, rwkv_in,
     *merge_gates) = [_proj(xn, w).reshape(b, n, w.shape[1]) for w in lp["w_in"]]
    y_a, lru_conv_new, lru_h_new = _rglru_mixer(lru_x, lru_gate, st["lru_conv"], st["lru_h"], lp)
    pos = q_off + jnp.arange(n)
    qn = lp["qk_norm"]
    q = _partial_rope(_rms_norm(q.reshape(b, n, NSA_HEADS, HEAD_DIM), qn[0]), pos)
    kv_cmp = kv_cmp.reshape(b, n, 2, NSA_KV, HEAD_DIM)
    kv_sel = kv_sel.reshape(b, n, 2, NSA_KV, HEAD_DIM)
    kv_win = kv_win.reshape(b, n, 2, NSA_KV, HEAD_DIM)
    k_sel = _partial_rope(_rms_norm(kv_sel[:, :, 0], qn[2]), pos)
    k_win = _partial_rope(_rms_norm(kv_win[:, :, 0], qn[3]), pos)
    nsa_rows = jnp.stack([kv_cmp[:, :, 0], kv_cmp[:, :, 1], k_sel, kv_sel[:, :, 1]], axis=2)
    win_rows = jnp.stack([k_win, kv_win[:, :, 1]], axis=2)
    win_all = jnp.concatenate([st["win"].astype(dt), win_rows], axis=1)
    win_new = win_all[:, win_all.shape[1] - win_keep:]
    nsa_gates = jax.nn.sigmoid(nsa_gate.reshape(b, n, NSA_HEADS, 3))
    if st["paged_past"] is None:
        assert q_off == 0 and st["win"].shape[1] == 0
        y_b = _nsa_mixer_prompt(q, nsa_rows, win_rows, nsa_gates, lp)
    else:
        cache, layer, page_table = st["paged_past"]
        assert q_off == page_table.shape[1] * PAGE_SIZE
        y_b = _nsa_mixer_paged(q, nsa_rows, win_all, nsa_gates, cache, layer, page_table, lp)
    y_c, rwkv_s_new, rwkv_shift_new = _rwkv7_mixer(rwkv_in, st["rwkv_shift"], st["rwkv_s"], lp)
    mix = 0.0
    for bi, y in enumerate((y_a, y_b, y_c)):
        pb = _proj(y.reshape(t, BR_W), lp["w_branch"][bi]).reshape(b, n, D_MODEL)
        mix = mix + jax.nn.sigmoid(merge_gates[bi]) * pb
    x = x + _proj(mix.reshape(t, D_MODEL), lp["w_out"]).reshape(b, n, D_MODEL)
    xn2 = _rms_norm(x, lp["norm2"]).reshape(t, D_MODEL).astype(BF16)
    hg, hv = [_proj(xn2, w).reshape(b, n, D_FF) for w in lp["ffn_w_in"]]
    hc, ffn_conv_new = _causal_dwconv(hg, st["ffn_conv"], lp["ffn_conv_w"], lp["ffn_conv_b"])
    act = jax.nn.gelu(hc) * hv
    x = x + _proj(act.reshape(t, D_FF), lp["ffn_w_down"]).reshape(b, n, D_MODEL)
    new_state = {"nsa": nsa_rows, "win": win_new, "lru_h": lru_h_new.astype(dt),
                 "lru_conv": lru_conv_new.astype(dt), "rwkv_s": rwkv_s_new.astype(dt),
                 "rwkv_shift": rwkv_shift_new.astype(dt), "ffn_conv": ffn_conv_new.astype(dt)}
    return x, new_state


def kernel(x_prompt, x_sample, cache_nsa_kv, state_win_kv, state_lru_h, state_lru_conv,
           state_rwkv_s, state_rwkv_shift, state_ffn_conv, page_table,
           norm1_g, norm2_g, w_in, lru_conv_w, lru_conv_b, lru_wa, lru_ba, lru_wx, lru_bx,
           lru_lambda, qk_norm_g, cmp_pe, cmp_w1, cmp_w2, rwkv_mu, rwkv_w0, rwkv_w2,
           rwkv_a0, rwkv_a2, rwkv_g2, rwkv_kk, rwkv_ka, rwkv_rk, rwkv_gn_g, rwkv_gn_b,
           w_branch, w_out, ffn_w_in, ffn_conv_w, ffn_conv_b, ffn_w_down):
    dt = x_prompt.dtype
    bp, n_p, _ = x_prompt.shape
    bs = x_sample.shape[0]
    depth = w_in.shape[0]
    past_len = page_table.shape[1] * PAGE_SIZE
    xp, xs = x_prompt, x_sample
    new_p = {name: [] for name in STATE_NAMES}
    new_s = {name: [] for name in STATE_NAMES}
    for l in range(depth):
        in_bounds = (0,) + IN_OFFSETS + tuple(IN_OFFSETS[-1] + i * D_MODEL for i in (1, 2)) + (D_IN,)
        w_in_segs = [w_in[l][:, a:z].astype(BF16) for a, z in zip(in_bounds[:-1], in_bounds[1:])]
        lp = {"norm1": norm1_g[l], "norm2": norm2_g[l], "w_in": w_in_segs,
              "lru_conv_w": lru_conv_w[l], "lru_conv_b": lru_conv_b[l], "lru_wa": lru_wa[l],
              "lru_ba": lru_ba[l], "lru_wx": lru_wx[l], "lru_bx": lru_bx[l],
              "lru_lambda": lru_lambda[l], "qk_norm": qk_norm_g[l], "cmp_pe": cmp_pe[l],
              "cmp_w1": cmp_w1[l], "cmp_w2": cmp_w2[l], "rwkv_mu": rwkv_mu[l],
              "rwkv_w0": rwkv_w0[l], "rwkv_w2": rwkv_w2[l], "rwkv_a0": rwkv_a0[l],
              "rwkv_a2": rwkv_a2[l], "rwkv_g2": rwkv_g2[l], "rwkv_kk": rwkv_kk[l],
              "rwkv_ka": rwkv_ka[l], "rwkv_rk": rwkv_rk[l], "rwkv_gn_g": rwkv_gn_g[l],
              "rwkv_gn_b": rwkv_gn_b[l], "w_branch": w_branch[l].astype(BF16),
              "w_out": w_out[l].astype(BF16), "ffn_w_in": [ffn_w_in[l][:, :D_FF].astype(BF16), ffn_w_in[l][:, D_FF:].astype(BF16)],
              "ffn_conv_w": ffn_conv_w[l], "ffn_conv_b": ffn_conv_b[l],
              "ffn_w_down": ffn_w_down[l].astype(BF16)}
        st_p = {"paged_past": None,
                "win": jnp.zeros((bp, 0, 2, NSA_KV, HEAD_DIM), dt),
                "lru_h": jnp.zeros((bp, BR_W), dt),
                "lru_conv": jnp.zeros((bp, LRU_CONV - 1, BR_W), dt),
                "rwkv_s": jnp.zeros((bp, RWKV_HEADS, RWKV_HD, RWKV_HD), dt),
                "rwkv_shift": jnp.zeros((bp, RWKV_IN), dt),
                "ffn_conv": jnp.zeros((bp, FFN_CONV - 1, D_FF), dt)}
        xp, out_p = _trunk_layer(xp, lp, st_p, 0, min(WINDOW, n_p))
        st_s = {"paged_past": (cache_nsa_kv, l, page_table), "win": state_win_kv[l], "lru_h": state_lru_h[l],
                "lru_conv": state_lru_conv[l], "rwkv_s": state_rwkv_s[l],
                "rwkv_shift": state_rwkv_shift[l], "ffn_conv": state_ffn_conv[l]}
        xs, out_s = _trunk_layer(xs, lp, st_s, past_len, state_win_kv.shape[2])
        for name in STATE_NAMES:
            new_p[name].append(out_p[name])
            new_s[name].append(out_s[name])
    sp = {name: jnp.stack(new_p[name], axis=0) for name in STATE_NAMES}
    ss = {name: jnp.stack(new_s[name], axis=0) for name in STATE_NAMES}
    return (xp, xs, sp["nsa"], sp["win"], sp["lru_h"], sp["lru_conv"], sp["rwkv_s"],
            sp["rwkv_shift"], sp["ffn_conv"], ss["nsa"], ss["win"], ss["lru_h"], ss["lru_conv"],
            ss["rwkv_s"], ss["rwkv_shift"], ss["ffn_conv"])
```
